```python
import math
import jax, jax.numpy as jnp
from jax import lax
import numpy as np

D_MODEL = 1024
BATCH = 8
SEQ = 4096
DEPTH = 4
DEC_BATCH = 32
DEC_SEQ = 16
PAST_LEN = 1024

CHUNK = 64
Q_BLOCK = 128
D_POOL = D_MODEL // 2
POOL_WINDOWS = (2, 4, 8, 16)
N_POOL_GROUPS = 4
POOL_GROUP = D_POOL // N_POOL_GROUPS
POOL_HIST = max(POOL_WINDOWS) - 1
N_HEADS = 8
HEAD_DIM = 64
QK_DIM = 2 * HEAD_DIM
V_DIM = 2 * HEAD_DIM
Q_WIDTH = N_HEADS * QK_DIM
D_ATT = N_HEADS * V_DIM
ROPE_THETA = 10000.0
LN_EPS = 1e-5
RMS_EPS = 1e-5
ALPHA = (2 * DEPTH) ** 0.25
BETA = (8 * DEPTH) ** -0.25
IN_WIDTHS = (D_POOL, D_POOL, Q_WIDTH, Q_WIDTH, D_ATT, D_ATT, D_MODEL, D_MODEL)
IN_SPLITS = tuple(int(s) for s in np.cumsum(IN_WIDTHS)[:-1])
D_IN = sum(IN_WIDTHS)

kernel_name = "pool_diffattn_deepnorm_stream_step"


def layer_norm(x, g, b):
    xf = x.astype(jnp.float32)
    mu = jnp.mean(xf, -1, keepdims=True)
    var = jnp.mean(jnp.square(xf - mu), -1, keepdims=True)
    y = (xf - mu) * lax.rsqrt(var + LN_EPS) * g.astype(jnp.float32) + b.astype(jnp.float32)
    return y.astype(x.dtype)


def rope(x, pos):
    half = HEAD_DIM // 2
    inv = ROPE_THETA ** (-jnp.arange(half, dtype=jnp.float32) / half)
    ang = pos.astype(jnp.float32)[:, None] * inv[None, :]
    shp = (ang.shape[0],) + (1,) * (x.ndim - 3) + (half,)
    cos = jnp.cos(ang).reshape(shp)
    sin = jnp.sin(ang).reshape(shp)
    xf = x.astype(jnp.float32)
    x1, x2 = xf[..., :half], xf[..., half:]
    return jnp.concatenate([x1 * cos - x2 * sin, x2 * cos + x1 * sin], -1).astype(x.dtype)


def pool_mix(u, hist, pos, w_pool, pool_scale):
    T = u.shape[1]
    ext = jnp.concatenate([hist.astype(u.dtype), u], 1)
    ef = ext.astype(jnp.float32)
    cs = jnp.concatenate([jnp.zeros_like(ef[:, :1]), lax.cumsum(ef, axis=1)], 1)
    uf = u.astype(jnp.float32)
    outs = []
    for g, w in enumerate(POOL_WINDOWS):
        sl = slice(g * POOL_GROUP, (g + 1) * POOL_GROUP)
        s = cs[:, POOL_HIST + 1:POOL_HIST + 1 + T, sl] - cs[:, POOL_HIST + 1 - w:POOL_HIST + 1 - w + T, sl]
        cnt = jnp.minimum(pos + 1, w).astype(jnp.float32)[None, :, None]
        pooled = (s / cnt - uf[..., sl]).astype(u.dtype)
        outs.append(jnp.einsum('btc,cd->btd', pooled, w_pool[g]))
    y = jnp.concatenate(outs, -1) * pool_scale
    return y, ext[:, -POOL_HIST:]


def diff_attend(q, k, v, lam, mask):
    s = jnp.einsum('bqhmd,bkhmd->bhmqk', q, k).astype(jnp.float32) * (HEAD_DIM ** -0.5)
    if mask is not None:
        s = jnp.where(mask, s, -jnp.inf)
    p = jax.nn.softmax(s, axis=-1)
    a = p[:, :, 0] - lam * p[:, :, 1]
    return jnp.einsum('bhqk,bkhv->bqhv', a.astype(v.dtype), v)


def prompt_diff_attention(q, k, v, lam):
    B, T = q.shape[0], q.shape[1]
    nb = T // Q_BLOCK
    qb = q.reshape(B, nb, Q_BLOCK, N_HEADS, 2, HEAD_DIM).transpose(1, 0, 2, 3, 4, 5)
    k_chunk = jnp.arange(T) // CHUNK

    def block(args):
        qi, i = args
        q_chunk = (i * Q_BLOCK + jnp.arange(Q_BLOCK)) // CHUNK
        mask = k_chunk[None, :] <= q_chunk[:, None]
        return diff_attend(qi, k, v, lam, mask)

    o = lax.map(block, (qb, jnp.arange(nb)))
    return o.transpose(1, 0, 2, 3, 4).reshape(B, T, N_HEADS, V_DIM)


def trunk_layer(x, pos, pool_hist, k_past, v_past, w_in, w_pool, pool_scale, lambda_qk,
                subln_w, w_a, w_b, w_o, ln_g, ln_b, layer_idx):
    B, T, _ = x.shape
    lam_init = 0.8 - 0.6 * math.exp(-0.3 * layer_idx)
    lq = lambda_qk.astype(jnp.float32)
    lam = jnp.exp(jnp.sum(lq[0] * lq[1])) - jnp.exp(jnp.sum(lq[2] * lq[3])) + lam_init

    h = jnp.einsum('btd,de->bte', x, w_in)
    px, pg, q, k, v, ag, ga, gb = jnp.split(h, IN_SPLITS, axis=-1)

    ya, new_hist = pool_mix(px, pool_hist, pos, w_pool, pool_scale)
    ya = ya * jax.nn.silu(pg)

    q = rope(q.reshape(B, T, N_HEADS, 2, HEAD_DIM), pos)
    k = rope(k.reshape(B, T, N_HEADS, 2, HEAD_DIM), pos)
    v = v.reshape(B, T, N_HEADS, V_DIM)
    if k_past is None:
        o = prompt_diff_attention(q, k, v, lam)
    else:
        P = k_past.shape[1]
        kc = jnp.concatenate([k_past.reshape(B, P, N_HEADS, 2, HEAD_DIM).astype(k.dtype), k], 1)
        vc = jnp.concatenate([v_past.astype(v.dtype), v], 1)
        o = diff_attend(q, kc, vc, lam, None)
    of = o.astype(jnp.float32)
    of = of * lax.rsqrt(jnp.mean(jnp.square(of), -1, keepdims=True) + RMS_EPS)
    of = of * subln_w.astype(jnp.float32) * (1.0 - lam_init)
    yb = of.astype(x.dtype).reshape(B, T, D_ATT) * jax.nn.silu(ag)

    merged = (jax.nn.sigmoid(ga) * jnp.einsum('btc,cd->btd', ya, w_a)
              + jax.nn.sigmoid(gb) * jnp.einsum('btc,cd->btd', yb, w_b))
    out = jnp.einsum('btd,de->bte', merged, w_o)
    x_new = layer_norm(ALPHA * x + out, ln_g, ln_b)
    return x_new, k.reshape(B, T, N_HEADS, QK_DIM), v, new_hist


def setup_inputs(seed: int = 0) -> dict:
    key = jax.random.key(seed)
    ks = jax.random.split(key, 20)
    f32 = jnp.float32
    nrm = lambda k, shp: jax.random.normal(k, shp, f32)
    return {
        'x_prompt': nrm(ks[0], (BATCH, SEQ, D_MODEL)),
        'x_sample': nrm(ks[1], (DEC_BATCH, DEC_SEQ, D_MODEL)),
        'cache_k': nrm(ks[2], (DEPTH, DEC_BATCH, PAST_LEN, N_HEADS, QK_DIM)),
        'cache_v': nrm(ks[3], (DEPTH, DEC_BATCH, PAST_LEN, N_HEADS, V_DIM)),
        'state_pool': nrm(ks[4], (DEPTH, DEC_BATCH, POOL_HIST, D_POOL)),
        'ln_in_g': 1.0 + 0.02 * nrm(ks[5], (D_MODEL,)),
        'ln_in_b': 0.02 * nrm(ks[6], (D_MODEL,)),
        'w_in': nrm(ks[7], (DEPTH, D_MODEL, D_IN)) * D_MODEL ** -0.5,
        'w_pool': nrm(ks[8], (DEPTH, N_POOL_GROUPS, POOL_GROUP, POOL_GROUP)) * POOL_GROUP ** -0.5,
        'pool_scale': 1.0 + 0.02 * nrm(ks[9], (DEPTH, D_POOL)),
        'lambda_qk': 0.1 * nrm(ks[10], (DEPTH, 4, HEAD_DIM)),
        'subln_w': 1.0 + 0.02 * nrm(ks[11], (DEPTH, V_DIM)),
        'w_a': nrm(ks[12], (DEPTH, D_POOL, D_MODEL)) * (D_POOL ** -0.5) * BETA,
        'w_b': nrm(ks[13], (DEPTH, D_ATT, D_MODEL)) * (D_ATT ** -0.5) * BETA,
        'w_o': nrm(ks[14], (DEPTH, D_MODEL, D_MODEL)) * (D_MODEL ** -0.5) * BETA,
        'ln_g': 1.0 + 0.02 * nrm(ks[15], (DEPTH, D_MODEL)),
        'ln_b': 0.02 * nrm(ks[16], (DEPTH, D_MODEL)),
    }


def reference(x_prompt, x_sample, cache_k, cache_v, state_pool, ln_in_g, ln_in_b, w_in, w_pool,
              pool_scale, lambda_qk, subln_w, w_a, w_b, w_o, ln_g, ln_b):
    xp = layer_norm(x_prompt, ln_in_g, ln_in_b)
    xs = layer_norm(x_sample, ln_in_g, ln_in_b)
    pos_p = jnp.arange(x_prompt.shape[1])
    pos_s = cache_k.shape[2] + jnp.arange(x_sample.shape[1])
    hist_p = jnp.zeros((x_prompt.shape[0], POOL_HIST, D_POOL), x_prompt.dtype)
    kp_l, vp_l, hp_l, ks_l, vs_l, hs_l = [], [], [], [], [], []
    for l in range(DEPTH):
        xp, kp, vp, hp = trunk_layer(xp, pos_p, hist_p, None, None, w_in[l], w_pool[l],
                                     pool_scale[l], lambda_qk[l], subln_w[l], w_a[l], w_b[l],
                                     w_o[l], ln_g[l], ln_b[l], l)
        xs, ks, vs, hs = trunk_layer(xs, pos_s, state_pool[l], cache_k[l], cache_v[l], w_in[l],
                                     w_pool[l], pool_scale[l], lambda_qk[l], subln_w[l], w_a[l],
                                     w_b[l], w_o[l], ln_g[l], ln_b[l], l)
        kp_l.append(kp); vp_l.append(vp); hp_l.append(hp)
        ks_l.append(ks); vs_l.append(vs); hs_l.append(hs)
    return (xp, xs, jnp.stack(kp_l), jnp.stack(vp_l), jnp.stack(hp_l),
            jnp.stack(ks_l), jnp.stack(vs_l), jnp.stack(hs_l))
```

```python
import functools
import math

import jax
import jax.numpy as jnp
from jax import lax
from jax.experimental import pallas as pl
from jax.experimental.pallas import tpu as pltpu

F32 = jnp.float32
BF16 = jnp.bfloat16

D_MODEL = 1024
DEPTH = 4
CHUNK = 64
CHUNK_SHIFT = 6
D_POOL = 512
POOL_WINDOWS = (2, 4, 8, 16)
POOL_GROUP = 128
POOL_HIST = 15
HIST_ROWS = 16
N_HEADS = 8
HEAD_DIM = 64
HEAD_W = 2 * HEAD_DIM
ROPE_HALF = HEAD_DIM // 2
ROPE_THETA = 10000.0
QK_SCALE = HEAD_DIM ** -0.5
LN_EPS = 1e-5
RMS_EPS = 1e-5
ALPHA = (2 * DEPTH) ** 0.25
COL_PX, COL_PG, COL_Q, COL_K, COL_V, COL_AG, COL_GA, COL_GB = 0, 512, 1024, 2048, 3072, 4096, 5120, 6144
D_IN = 7168

V7X_VMEM_BYTES = 64 * 1024 * 1024
LANES = 128

MASKED = -1e30
ROW_TILE = 512
ATTN_TILE = 256


TEMP_VMEM_BYTES = 16 << 20


def _vmem_limit(pipelined_bytes, resident_bytes=0):
    return min(2 * pipelined_bytes + resident_bytes + TEMP_VMEM_BYTES, V7X_VMEM_BYTES - (4 << 20))


def _sigmoid(x):
    return 1.0 / (1.0 + jnp.exp(-x))


def _silu(x):
    return x * _sigmoid(x)


def _layer_norm_rows(z, g, b):
    mu = jnp.mean(z, axis=-1, keepdims=True)
    zc = z - mu
    var = jnp.mean(zc * zc, axis=-1, keepdims=True)
    return zc * lax.rsqrt(var + LN_EPS) * g + b


def _ln_kernel(x_ref, g_ref, b_ref, y_ref):
    y_ref[...] = _layer_norm_rows(x_ref[...], g_ref[...], b_ref[...])


def _entry_layer_norm(x2d, g, b):
    rows = x2d.shape[0]
    tm = min(ROW_TILE, rows)
    row_spec = pl.BlockSpec((tm, D_MODEL), lambda i: (i, 0))
    vec_spec = pl.BlockSpec((1, D_MODEL), lambda i: (0, 0))
    return pl.pallas_call(
        _ln_kernel,
        grid=(rows // tm,),
        in_specs=[row_spec, vec_spec, vec_spec],
        out_specs=row_spec,
        out_shape=jax.ShapeDtypeStruct((rows, D_MODEL), F32),
        compiler_params=pltpu.CompilerParams(
            dimension_semantics=("arbitrary",),
            vmem_limit_bytes=_vmem_limit(2 * tm * D_MODEL * 4)),
        name="entry_ln",
    )(x2d, g.reshape(1, D_MODEL), b.reshape(1, D_MODEL))


def _rope_table_kernel(cos_ref, sin_ref, *, pos0, period, tm):
    i = pl.program_id(0)
    lane = lax.broadcasted_iota(jnp.int32, (tm, LANES), 1)
    row = lax.broadcasted_iota(jnp.int32, (tm, LANES), 0) + i * tm
    freq = (lane & (ROPE_HALF - 1)).astype(F32)
    inv = jnp.exp(freq * (-math.log(ROPE_THETA) / ROPE_HALF))
    pos = (pos0 + (row & (period - 1))).astype(F32)
    ang = pos * inv
    sign = jnp.where((lane & ROPE_HALF) == 0, -1.0, 1.0)
    cos_ref[...] = jnp.cos(ang)
    sin_ref[...] = jnp.sin(ang) * sign


def _rope_tables(rows, pos0, period):
    tm = min(ROW_TILE, rows)
    spec = pl.BlockSpec((tm, LANES), lambda i: (i, 0))
    shape = jax.ShapeDtypeStruct((rows, LANES), F32)
    return pl.pallas_call(
        functools.partial(_rope_table_kernel, pos0=pos0, period=period, tm=tm),
        grid=(rows // tm,),
        out_specs=[spec, spec],
        out_shape=[shape, shape],
        compiler_params=pltpu.CompilerParams(dimension_semantics=("arbitrary",)),
        name="rope_tables",
    )()


def _inproj_kernel(x_ref, w_ref, cos_ref, sin_ref,
                   px_ref, pg_ref, q_ref, k_ref, kb_ref, v_ref, vb_ref, ag_ref, ga_ref, gb_ref,
                   xb_ref):
    tm = x_ref.shape[0]
    xb_ref[...] = x_ref[...].astype(BF16)

    def proj(col, width):
        return jnp.dot(xb_ref[...], w_ref[:, col:col + width], preferred_element_type=F32)

    px_ref[...] = proj(COL_PX, D_POOL)
    pg_ref[...] = proj(COL_PG, D_POOL).astype(BF16)

    cos = cos_ref[...]
    sin = sin_ref[...]
    lane = lax.broadcasted_iota(jnp.int32, (tm, LANES), 1)
    holds_x1 = (lane & ROPE_HALF) == 0

    def rope(y):
        partner = jnp.where(holds_x1,
                            pltpu.roll(y, LANES - ROPE_HALF, 1),
                            pltpu.roll(y, ROPE_HALF, 1))
        return y * cos + partner * sin

    q = proj(COL_Q, D_MODEL)
    for h in range(N_HEADS):
        sl = slice(h * HEAD_W, (h + 1) * HEAD_W)
        q_ref[:, sl] = (rope(q[:, sl]) * QK_SCALE).astype(BF16)
    k = proj(COL_K, D_MODEL)
    for h in range(N_HEADS):
        sl = slice(h * HEAD_W, (h + 1) * HEAD_W)
        kr = rope(k[:, sl])
        k_ref[:, sl] = kr
        kb_ref[:, sl] = kr.astype(BF16)
    v = proj(COL_V, D_MODEL)
    v_ref[...] = v
    vb_ref[...] = v.astype(BF16)
    ag_ref[...] = proj(COL_AG, D_MODEL).astype(BF16)
    ga_ref[...] = proj(COL_GA, D_MODEL).astype(BF16)
    gb_ref[...] = proj(COL_GB, D_MODEL).astype(BF16)


def _in_projection(x2d, w_bf16, cos_t, sin_t):
    rows = x2d.shape[0]
    tm = min(ROW_TILE, rows)
    table_blocks = cos_t.shape[0] // tm

    def rows_spec(width):
        return pl.BlockSpec((tm, width), lambda i: (i, 0))

    table_spec = pl.BlockSpec((tm, LANES), lambda i: (i % table_blocks, 0))
    w_spec = pl.BlockSpec((D_MODEL, D_IN), lambda i: (0, 0), pipeline_mode=pl.Buffered(1))
    outs = [(D_POOL, F32), (D_POOL, BF16), (D_MODEL, BF16), (D_MODEL, F32), (D_MODEL, BF16),
            (D_MODEL, F32), (D_MODEL, BF16), (D_MODEL, BF16), (D_MODEL, BF16), (D_MODEL, BF16)]
    out_row_bytes = sum(wd * jnp.dtype(dt).itemsize for wd, dt in outs)
    block_bytes = tm * (D_MODEL * 4 + 2 * LANES * 4 + out_row_bytes)
    return pl.pallas_call(
        _inproj_kernel,
        grid=(rows // tm,),
        in_specs=[rows_spec(D_MODEL), w_spec, table_spec, table_spec],
        out_specs=[rows_spec(wd) for wd, _ in outs],
        out_shape=[jax.ShapeDtypeStruct((rows, wd), dt) for wd, dt in outs],
        scratch_shapes=[pltpu.VMEM((tm, D_MODEL), BF16)],
        compiler_params=pltpu.CompilerParams(
            dimension_semantics=("arbitrary",),
            vmem_limit_bytes=_vmem_limit(block_bytes, D_MODEL * D_IN * 2 + tm * D_MODEL * 2)),
        name="in_projection",
    )(x2d, w_bf16, cos_t, sin_t)


def _stack_maps(q, q2_ref):
    t = q.shape[0]
    lane = lax.broadcasted_iota(jnp.int32, q.shape, 1)
    zero = jnp.zeros_like(q)
    q2_ref[0:t, :] = jnp.where(lane < HEAD_DIM, q, zero)
    q2_ref[t:2 * t, :] = jnp.where(lane >= HEAD_DIM, q, zero)
    return q2_ref[...]


def _scores(q2, k_blk):
    return lax.dot_general(q2, k_blk, (((1,), (1,)), ((), ())), preferred_element_type=F32)


def _lambda(lamqk_ref, lam_init):
    lq = lamqk_ref[...]
    d1 = jnp.sum(lq[0:1, :] * lq[1:2, :], axis=1, keepdims=True)
    d2 = jnp.sum(lq[2:3, :] * lq[3:4, :], axis=1, keepdims=True)
    return jnp.exp(d1) - jnp.exp(d2) + lam_init


def _finish_head(acc, l, t, lam, subw, lam_init):
    o1 = acc[0:t, :] / l[0:t, :]
    o2 = acc[t:2 * t, :] / l[t:2 * t, :]
    o = o1 - lam * o2
    ms = jnp.mean(o * o, axis=1, keepdims=True)
    return o * lax.rsqrt(ms + RMS_EPS) * subw * (1.0 - lam_init)


def _prompt_attn_kernel(lamqk_ref, subw_ref, q_ref, k_ref, v_ref, o_ref,
                        q2_ref, m_ref, l_ref, acc_ref, *, t, lam_init):
    qi = pl.program_id(2)
    q2 = _stack_maps(q_ref[0], q2_ref)

    start = pl.multiple_of(qi * t, t)
    s = _scores(q2, k_ref[0, pl.ds(start, t), :])
    r = lax.broadcasted_iota(jnp.int32, (2 * t, t), 0)
    c = lax.broadcasted_iota(jnp.int32, (2 * t, t), 1)
    visible = (c >> CHUNK_SHIFT) <= ((r & (t - 1)) >> CHUNK_SHIFT)
    s = jnp.where(visible, s, MASKED)
    m = jnp.max(s, axis=1, keepdims=True)
    p = jnp.exp(s - m)
    m_ref[...] = m
    l_ref[...] = jnp.sum(p, axis=1, keepdims=True)
    acc_ref[...] = jnp.dot(p.astype(BF16), v_ref[0, pl.ds(start, t), :], preferred_element_type=F32)

    def body(j, carry):
        off = pl.multiple_of(j * t, t)
        s = _scores(q2, k_ref[0, pl.ds(off, t), :])
        m_prev = m_ref[...]
        m_new = jnp.maximum(m_prev, jnp.max(s, axis=1, keepdims=True))
        alpha = jnp.exp(m_prev - m_new)
        p = jnp.exp(s - m_new)
        m_ref[...] = m_new
        l_ref[...] = alpha * l_ref[...] + jnp.sum(p, axis=1, keepdims=True)
        acc_ref[...] = alpha * acc_ref[...] + jnp.dot(
            p.astype(BF16), v_ref[0, pl.ds(off, t), :], preferred_element_type=F32)
        return carry

    lax.fori_loop(0, qi, body, 0)

    lam = _lambda(lamqk_ref, lam_init)
    o_ref[0] = _finish_head(acc_ref[...], l_ref[...], t, lam, subw_ref[...], lam_init).astype(BF16)


def _prompt_attention(q, kb, vb, lamqk, subw, lam_init):
    bsz, seq, width = q.shape
    t = ATTN_TILE
    q_spec = pl.BlockSpec((1, t, HEAD_W), lambda b, h, i: (b, i, h))
    kv_spec = pl.BlockSpec((1, seq, HEAD_W), lambda b, h, i: (b, 0, h))
    block_bytes = 2 * t * HEAD_W * 2 + 2 * seq * HEAD_W * 2
    return pl.pallas_call(
        functools.partial(_prompt_attn_kernel, t=t, lam_init=lam_init),
        grid=(bsz, N_HEADS, seq // t),
        in_specs=[pl.BlockSpec((4, HEAD_DIM), lambda b, h, i: (0, 0)),
                  pl.BlockSpec((1, HEAD_W), lambda b, h, i: (0, 0)),
                  q_spec, kv_spec, kv_spec],
        out_specs=q_spec,
        out_shape=jax.ShapeDtypeStruct((bsz, seq, width), BF16),
        scratch_shapes=[pltpu.VMEM((2 * t, HEAD_W), BF16),
                        pltpu.VMEM((2 * t, 1), F32),
                        pltpu.VMEM((2 * t, 1), F32),
                        pltpu.VMEM((2 * t, HEAD_W), F32)],
        compiler_params=pltpu.CompilerParams(
            dimension_semantics=("arbitrary", "arbitrary", "arbitrary"),
            vmem_limit_bytes=_vmem_limit(block_bytes)),
        name="prompt_attention",
    )(lamqk, subw, q, kb, vb)


def _sample_attn_kernel(lamqk_ref, subw_ref, q_ref, kn_ref, vn_ref, kc_ref, vc_ref, o_ref,
                        q2_ref, *, t, lam_init):
    lam = _lambda(lamqk_ref, lam_init)
    subw = subw_ref[...]
    for h in range(N_HEADS):
        sl = slice(h * HEAD_W, (h + 1) * HEAD_W)
        q2 = _stack_maps(q_ref[0, :, sl], q2_ref)
        s_c = _scores(q2, kc_ref[0, 0, :, sl].astype(BF16))
        s_n = _scores(q2, kn_ref[0, :, sl])
        m = jnp.maximum(jnp.max(s_c, axis=1, keepdims=True), jnp.max(s_n, axis=1, keepdims=True))
        p_c = jnp.exp(s_c - m)
        p_n = jnp.exp(s_n - m)
        l = jnp.sum(p_c, axis=1, keepdims=True) + jnp.sum(p_n, axis=1, keepdims=True)
        acc = (jnp.dot(p_c.astype(BF16), vc_ref[0, 0, :, sl].astype(BF16), preferred_element_type=F32)
               + jnp.dot(p_n.astype(BF16), vn_ref[0, :, sl], preferred_element_type=F32))
        o_ref[0, :, sl] = _finish_head(acc, l, t, lam, subw, lam_init).astype(BF16)


def _sample_attention(q, kb, vb, cache_k4, cache_v4, layer, lamqk, subw, lam_init):
    bsz, t, width = q.shape
    past = cache_k4.shape[2]
    new_spec = pl.BlockSpec((1, t, width), lambda b: (b, 0, 0))
    cache_spec = pl.BlockSpec((1, 1, past, width), lambda b: (layer, b, 0, 0))
    block_bytes = 4 * t * width * 2 + 2 * past * width * 4
    return pl.pallas_call(
        functools.partial(_sample_attn_kernel, t=t, lam_init=lam_init),
        grid=(bsz,),
        in_specs=[pl.BlockSpec((4, HEAD_DIM), lambda b: (0, 0)),
                  pl.BlockSpec((1, HEAD_W), lambda b: (0, 0)),
                  new_spec, new_spec, new_spec, cache_spec, cache_spec],
        out_specs=new_spec,
        out_shape=jax.ShapeDtypeStruct((bsz, t, width), BF16),
        scratch_shapes=[pltpu.VMEM((2 * t, HEAD_W), BF16)],
        compiler_params=pltpu.CompilerParams(
            dimension_semantics=("arbitrary",),
            vmem_limit_bytes=_vmem_limit(block_bytes)),
        name="sample_attention",
    )(lamqk, subw, q, kb, vb, cache_k4, cache_v4)


def _out_kernel(px_ref, hist_ref, pg_ref, o_ref, ag_ref, ga_ref, gb_ref, x_ref,
                wp_ref, ps_ref, wa_ref, wb_ref, wo_ref, g_ref, b_ref,
                y_ref, ext_ref, *, nb, tm, pos0, first_tile_has_no_history):
    i = pl.program_id(1)
    rows = nb * tm
    hist = hist_ref[...]
    if first_tile_has_no_history:
        hist = jnp.where(i == 0, 0.0, hist)
    ext_ref[:, 0:HIST_ROWS, :] = hist
    ext_ref[:, HIST_ROWS:HIST_ROWS + tm, :] = px_ref[...]

    pos = pos0 + i * tm + lax.broadcasted_iota(jnp.int32, (nb, tm, POOL_GROUP), 1)
    ya_parts = []
    for g, w in enumerate(POOL_WINDOWS):
        sl = slice(g * POOL_GROUP, (g + 1) * POOL_GROUP)
        u = ext_ref[:, HIST_ROWS:HIST_ROWS + tm, sl]
        s = u
        for j in range(1, w):
            s = s + ext_ref[:, HIST_ROWS - j:HIST_ROWS - j + tm, sl]
        cnt = jnp.minimum(pos + 1, w).astype(F32)
        pooled = (s / cnt - u).reshape(rows, POOL_GROUP)
        ya_parts.append(jnp.dot(pooled.astype(BF16), wp_ref[g], preferred_element_type=F32))
    ya = jnp.concatenate(ya_parts, axis=1) * ps_ref[...]
    ya = ya * _silu(pg_ref[...].reshape(rows, D_POOL).astype(F32))

    yb = o_ref[...].reshape(rows, D_MODEL).astype(F32) * _silu(ag_ref[...].reshape(rows, D_MODEL).astype(F32))

    ma = jnp.dot(ya.astype(BF16), wa_ref[...], preferred_element_type=F32)
    mb = jnp.dot(yb.astype(BF16), wb_ref[...], preferred_element_type=F32)
    merged = (_sigmoid(ga_ref[...].reshape(rows, D_MODEL).astype(F32)) * ma
              + _sigmoid(gb_ref[...].reshape(rows, D_MODEL).astype(F32)) * mb)
    out = jnp.dot(merged.astype(BF16), wo_ref[...], preferred_element_type=F32)
    z = ALPHA * x_ref[...].reshape(rows, D_MODEL) + out
    y_ref[...] = _layer_norm_rows(z, g_ref[...], b_ref[...]).reshape(nb, tm, D_MODEL)


def _out_projection(px, hist, pg, o, ag, ga, gb, x, wp, ps, wa, wb, wo, g, b, *,
                    nb, tm, pos0, first_tile_has_no_history):
    bsz, seq, _ = px.shape
    hist_blocks_per_tile = tm // HIST_ROWS

    def act_spec(width):
        return pl.BlockSpec((nb, tm, width), lambda bi, i: (bi, i, 0))

    def whole(shape):
        return pl.BlockSpec(shape, lambda bi, i: (0,) * len(shape))

    hist_spec = pl.BlockSpec(
        (nb, HIST_ROWS, D_POOL),
        lambda bi, i: (bi, jnp.maximum(i * hist_blocks_per_tile - 1, 0), 0))
    rows = nb * tm
    block_bytes = (rows * (D_POOL * 4 + D_POOL * 2 + 4 * D_MODEL * 2 + 2 * D_MODEL * 4)
                   + (D_POOL + 2 * D_MODEL) * D_MODEL * 2)
    return pl.pallas_call(
        functools.partial(_out_kernel, nb=nb, tm=tm, pos0=pos0,
                          first_tile_has_no_history=first_tile_has_no_history),
        grid=(bsz // nb, seq // tm),
        in_specs=[act_spec(D_POOL), hist_spec, act_spec(D_POOL), act_spec(D_MODEL), act_spec(D_MODEL),
                  act_spec(D_MODEL), act_spec(D_MODEL), act_spec(D_MODEL),
                  whole((len(POOL_WINDOWS), POOL_GROUP, POOL_GROUP)), whole((1, D_POOL)),
                  whole((D_POOL, D_MODEL)), whole((D_MODEL, D_MODEL)), whole((D_MODEL, D_MODEL)),
                  whole((1, D_MODEL)), whole((1, D_MODEL))],
        out_specs=act_spec(D_MODEL),
        out_shape=jax.ShapeDtypeStruct((bsz, seq, D_MODEL), F32),
        scratch_shapes=[pltpu.VMEM((nb, HIST_ROWS + tm, D_POOL), F32)],
        compiler_params=pltpu.CompilerParams(
            dimension_semantics=("arbitrary", "arbitrary"),
            vmem_limit_bytes=_vmem_limit(block_bytes)),
        name="out_projection",
    )(px, hist, pg, o, ag, ga, gb, x, wp, ps, wa, wb, wo, g, b)


def kernel(x_prompt, x_sample, cache_k, cache_v, state_pool, ln_in_g, ln_in_b, w_in, w_pool,
           pool_scale, lambda_qk, subln_w, w_a, w_b, w_o, ln_g, ln_b):
    bsz, seq, _ = x_prompt.shape
    dbsz, dseq, _ = x_sample.shape
    past = cache_k.shape[2]
    width = N_HEADS * HEAD_W

    cos_p, sin_p = _rope_tables(seq, 0, seq)
    cos_s, sin_s = _rope_tables(dbsz * dseq, past, dseq)

    xp = _entry_layer_norm(x_prompt.reshape(bsz * seq, D_MODEL), ln_in_g, ln_in_b)
    xs = _entry_layer_norm(x_sample.reshape(dbsz * dseq, D_MODEL), ln_in_g, ln_in_b)

    cache_k4 = cache_k.reshape(DEPTH, dbsz, past, width)
    cache_v4 = cache_v.reshape(DEPTH, dbsz, past, width)
    state16 = jnp.pad(state_pool, ((0, 0), (0, 0), (HIST_ROWS - POOL_HIST, 0), (0, 0)))

    kp_l, vp_l, hp_l, ks_l, vs_l, hs_l = [], [], [], [], [], []
    for l in range(DEPTH):
        lam_init = 0.8 - 0.6 * math.exp(-0.3 * l)
        w_in_b = w_in[l].astype(BF16)
        wp_b, wa_b, wb_b, wo_b = (w_pool[l].astype(BF16), w_a[l].astype(BF16),
                                  w_b[l].astype(BF16), w_o[l].astype(BF16))
        ps = pool_scale[l].reshape(1, D_POOL)
        subw = subln_w[l].reshape(1, HEAD_W)
        g = ln_g[l].reshape(1, D_MODEL)
        b = ln_b[l].reshape(1, D_MODEL)

        px, pg, q, k, kb, v, vb, ag, ga, gb = _in_projection(xp, w_in_b, cos_p, sin_p)
        r3 = lambda a: a.reshape(bsz, seq, a.shape[-1])
        o = _prompt_attention(r3(q), r3(kb), r3(vb), lambda_qk[l], subw, lam_init)
        px3 = r3(px)
        xp = _out_projection(px3, px3, r3(pg), o, r3(ag), r3(ga), r3(gb), r3(xp),
                             wp_b, ps, wa_b, wb_b, wo_b, g, b,
                             nb=1, tm=ROW_TILE, pos0=0,
                             first_tile_has_no_history=True).reshape(bsz * seq, D_MODEL)
        kp_l.append(k.reshape(bsz, seq, N_HEADS, HEAD_W))
        vp_l.append(v.reshape(bsz, seq, N_HEADS, HEAD_W))
        hp_l.append(px3[:, seq - POOL_HIST:, :])

        px, pg, q, k, kb, v, vb, ag, ga, gb = _in_projection(xs, w_in_b, cos_s, sin_s)
        s3 = lambda a: a.reshape(dbsz, dseq, a.shape[-1])
        o = _sample_attention(s3(q), s3(kb), s3(vb), cache_k4, cache_v4, l,
                              lambda_qk[l], subw, lam_init)
        px3 = s3(px)
        xs = _out_projection(px3, state16[l], s3(pg), o, s3(ag), s3(ga), s3(gb), s3(xs),
                             wp_b, ps, wa_b, wb_b, wo_b, g, b,
                             nb=dbsz, tm=dseq, pos0=past,
                             first_tile_has_no_history=False).reshape(dbsz * dseq, D_MODEL)
        ks_l.append(k.reshape(dbsz, dseq, N_HEADS, HEAD_W))
        vs_l.append(v.reshape(dbsz, dseq, N_HEADS, HEAD_W))
        hs_l.append(jnp.concatenate([state_pool[l], px3], axis=1)[:, -POOL_HIST:, :])

    return (xp.reshape(bsz, seq, D_MODEL), xs.reshape(dbsz, dseq, D_MODEL),
            jnp.stack(kp_l), jnp.stack(vp_l), jnp.stack(hp_l),
            jnp.stack(ks_l), jnp.stack(vs_l), jnp.stack(hs_l))
```

```python
import functools
import math

import jax
import jax.numpy as jnp
from jax import lax
from jax.experimental import pallas as pl
from jax.experimental.pallas import tpu as pltpu

F32 = jnp.float32
BF16 = jnp.bfloat16

D_MODEL = 1024
DEPTH = 4
CHUNK = 64
CHUNK_SHIFT = 6
D_POOL = 512
POOL_WINDOWS = (2, 4, 8, 16)
POOL_GROUP = 128
POOL_HIST = 15
HIST_ROWS = 16
N_HEADS = 8
HEAD_DIM = 64
HEAD_W = 2 * HEAD_DIM
ROPE_HALF = HEAD_DIM // 2
ROPE_THETA = 10000.0
QK_SCALE = HEAD_DIM ** -0.5
LN_EPS = 1e-5
RMS_EPS = 1e-5
ALPHA = (2 * DEPTH) ** 0.25
COL_PX, COL_PG, COL_Q, COL_K, COL_V, COL_AG, COL_GA, COL_GB = 0, 512, 1024, 2048, 3072, 4096, 5120, 6144
D_IN = 7168

V7X_VMEM_BYTES = 64 * 1024 * 1024
LANES = 128

MASKED = -1e30
ROW_TILE = 512
ATTN_TILE = 512
TEMP_VMEM_BYTES = 16 << 20


def _vmem_limit(pipelined_bytes, resident_bytes=0):
    return min(2 * pipelined_bytes + resident_bytes + TEMP_VMEM_BYTES, V7X_VMEM_BYTES - (4 << 20))


def _sigmoid(x):
    return 1.0 / (1.0 + jnp.exp(-x))


def _silu(x):
    return x * _sigmoid(x)


def _layer_norm_rows(z, g, b):
    mu = jnp.mean(z, axis=-1, keepdims=True)
    zc = z - mu
    var = jnp.mean(zc * zc, axis=-1, keepdims=True)
    return zc * lax.rsqrt(var + LN_EPS) * g + b


def _ln_kernel(x_ref, g_ref, b_ref, y_ref):
    y_ref[...] = _layer_norm_rows(x_ref[...], g_ref[...], b_ref[...])


def _entry_layer_norm(x2d, g, b):
    rows = x2d.shape[0]
    tm = min(ROW_TILE, rows)
    row_spec = pl.BlockSpec((tm, D_MODEL), lambda i: (i, 0))
    vec_spec = pl.BlockSpec((1, D_MODEL), lambda i: (0, 0))
    return pl.pallas_call(
        _ln_kernel,
        grid=(rows // tm,),
        in_specs=[row_spec, vec_spec, vec_spec],
        out_specs=row_spec,
        out_shape=jax.ShapeDtypeStruct((rows, D_MODEL), F32),
        compiler_params=pltpu.CompilerParams(
            dimension_semantics=("arbitrary",),
            vmem_limit_bytes=_vmem_limit(2 * tm * D_MODEL * 4)),
        name="entry_ln",
    )(x2d, g.reshape(1, D_MODEL), b.reshape(1, D_MODEL))


def _rope_table_kernel(cos_ref, sin_ref, *, pos0, period, tm):
    i = pl.program_id(0)
    lane = lax.broadcasted_iota(jnp.int32, (tm, LANES), 1)
    row = lax.broadcasted_iota(jnp.int32, (tm, LANES), 0) + i * tm
    freq = (lane & (ROPE_HALF - 1)).astype(F32)
    inv = jnp.exp(freq * (-math.log(ROPE_THETA) / ROPE_HALF))
    pos = (pos0 + (row & (period - 1))).astype(F32)
    ang = pos * inv
    sign = jnp.where((lane & ROPE_HALF) == 0, -1.0, 1.0)
    cos_ref[...] = jnp.cos(ang)
    sin_ref[...] = jnp.sin(ang) * sign


def _rope_tables(rows, pos0, period):
    tm = min(ROW_TILE, rows)
    spec = pl.BlockSpec((tm, LANES), lambda i: (i, 0))
    shape = jax.ShapeDtypeStruct((rows, LANES), F32)
    return pl.pallas_call(
        functools.partial(_rope_table_kernel, pos0=pos0, period=period, tm=tm),
        grid=(rows // tm,),
        out_specs=[spec, spec],
        out_shape=[shape, shape],
        compiler_params=pltpu.CompilerParams(dimension_semantics=("arbitrary",)),
        name="rope_tables",
    )()


def _inproj_kernel(layer_ref, x_ref, w_ref, cos_ref, sin_ref, k_all_in, v_all_in,
                   px_ref, pg_ref, q_ref, kb_ref, vb_ref, ag_ref, ga_ref, gb_ref, k_ref, v_ref,
                   xb_ref):
    del layer_ref, k_all_in, v_all_in
    tm = x_ref.shape[0]
    xb_ref[...] = x_ref[...].astype(BF16)

    def proj(col, width):
        return jnp.dot(xb_ref[...], w_ref[:, col:col + width], preferred_element_type=F32)

    def head_rows(h):
        return pl.ds(h, tm, stride=N_HEADS)

    px_ref[...] = proj(COL_PX, D_POOL)
    pg_ref[...] = proj(COL_PG, D_POOL).astype(BF16)

    cos = cos_ref[...]
    sin = sin_ref[...]
    lane = lax.broadcasted_iota(jnp.int32, (tm, LANES), 1)
    holds_x1 = (lane & ROPE_HALF) == 0

    def rope(y):
        partner = jnp.where(holds_x1,
                            pltpu.roll(y, LANES - ROPE_HALF, 1),
                            pltpu.roll(y, ROPE_HALF, 1))
        return y * cos + partner * sin

    q = proj(COL_Q, D_MODEL)
    for h in range(N_HEADS):
        sl = slice(h * HEAD_W, (h + 1) * HEAD_W)
        q_ref[:, sl] = (rope(q[:, sl]) * QK_SCALE).astype(BF16)
    k = proj(COL_K, D_MODEL)
    for h in range(N_HEADS):
        sl = slice(h * HEAD_W, (h + 1) * HEAD_W)
        kr = rope(k[:, sl])
        k_ref[0, head_rows(h), :] = kr
        kb_ref[:, sl] = kr.astype(BF16)
    v = proj(COL_V, D_MODEL)
    for h in range(N_HEADS):
        sl = slice(h * HEAD_W, (h + 1) * HEAD_W)
        v_ref[0, head_rows(h), :] = v[:, sl]
    vb_ref[...] = v.astype(BF16)
    ag_ref[...] = proj(COL_AG, D_MODEL).astype(BF16)
    ga_ref[...] = proj(COL_GA, D_MODEL).astype(BF16)
    gb_ref[...] = proj(COL_GB, D_MODEL).astype(BF16)


def _in_projection(layer, x2d, w_bf16, cos_t, sin_t, k_all, v_all):
    rows = x2d.shape[0]
    tm = min(ROW_TILE, rows)
    table_blocks = cos_t.shape[0] // tm

    def rows_spec(width):
        return pl.BlockSpec((tm, width), lambda i, layer_ref: (i, 0))

    table_spec = pl.BlockSpec((tm, LANES), lambda i, layer_ref: (i % table_blocks, 0))
    w_spec = pl.BlockSpec((D_MODEL, D_IN), lambda i, layer_ref: (0, 0), pipeline_mode=pl.Buffered(1))
    kv_out_spec = pl.BlockSpec((1, tm * N_HEADS, HEAD_W), lambda i, layer_ref: (layer_ref[0], i, 0))
    any_spec = pl.BlockSpec(memory_space=pl.ANY)
    outs = [(D_POOL, F32), (D_POOL, BF16), (D_MODEL, BF16), (D_MODEL, BF16), (D_MODEL, BF16),
            (D_MODEL, BF16), (D_MODEL, BF16), (D_MODEL, BF16)]
    out_row_bytes = sum(wd * jnp.dtype(dt).itemsize for wd, dt in outs) + 2 * D_MODEL * 4
    block_bytes = tm * (D_MODEL * 4 + 2 * LANES * 4 + out_row_bytes)
    res = pl.pallas_call(
        _inproj_kernel,
        grid_spec=pltpu.PrefetchScalarGridSpec(
            num_scalar_prefetch=1,
            grid=(rows // tm,),
            in_specs=[rows_spec(D_MODEL), w_spec, table_spec, table_spec, any_spec, any_spec],
            out_specs=[rows_spec(wd) for wd, _ in outs] + [kv_out_spec, kv_out_spec],
            scratch_shapes=[pltpu.VMEM((tm, D_MODEL), BF16)]),
        out_shape=[jax.ShapeDtypeStruct((rows, wd), dt) for wd, dt in outs]
        + [jax.ShapeDtypeStruct(k_all.shape, F32), jax.ShapeDtypeStruct(v_all.shape, F32)],
        input_output_aliases={5: 8, 6: 9},
        compiler_params=pltpu.CompilerParams(
            dimension_semantics=("arbitrary",),
            vmem_limit_bytes=_vmem_limit(block_bytes, D_MODEL * D_IN * 2 + tm * D_MODEL * 2)),
        name="in_projection",
    )(jnp.full((1,), layer, jnp.int32), x2d, w_bf16, cos_t, sin_t, k_all, v_all)
    return res


def _stack_maps(q, q2_ref):
    t = q.shape[0]
    lane = lax.broadcasted_iota(jnp.int32, q.shape, 1)
    zero = jnp.zeros_like(q)
    q2_ref[0:t, :] = jnp.where(lane < HEAD_DIM, q, zero)
    q2_ref[t:2 * t, :] = jnp.where(lane >= HEAD_DIM, q, zero)


def _scores(q2, k_blk):
    return lax.dot_general(q2, k_blk, (((1,), (1,)), ((), ())), preferred_element_type=F32)


def _lambda(lam_ref):
    lq = lam_ref[...]
    d1 = jnp.sum(lq[0:1, :] * lq[1:2, :], axis=1, keepdims=True)
    d2 = jnp.sum(lq[2:3, :] * lq[3:4, :], axis=1, keepdims=True)
    lam_init = lq[4:5, 0:1]
    return jnp.exp(d1) - jnp.exp(d2) + lam_init, lam_init


def _finish_head(acc, l, t, lam, lam_init, subw):
    o1 = acc[0:t, :] / l[0:t, :]
    o2 = acc[t:2 * t, :] / l[t:2 * t, :]
    o = o1 - lam * o2
    ms = jnp.mean(o * o, axis=1, keepdims=True)
    return o * lax.rsqrt(ms + RMS_EPS) * subw * (1.0 - lam_init)


def _prompt_attn_kernel(lam_ref, subw_ref, q_ref, k_ref, v_ref, o_ref,
                        q2_ref, m_ref, l_ref, acc_ref, *, t):
    qi = pl.program_id(2)
    _stack_maps(q_ref[0], q2_ref)
    m_ref[...] = jnp.full(m_ref.shape, MASKED, F32)
    l_ref[...] = jnp.zeros(l_ref.shape, F32)
    acc_ref[...] = jnp.zeros(acc_ref.shape, F32)

    def block_step(off, width, hidden):
        s = _scores(q2_ref[...], k_ref[0, pl.ds(off, width), :])
        if hidden is not None:
            s = jnp.where(hidden, MASKED, s)
        slabs = [s[:, cb * LANES:(cb + 1) * LANES] for cb in range(width // LANES)]
        top = slabs[0]
        for sb in slabs[1:]:
            top = jnp.maximum(top, sb)
        m_prev = m_ref[...]
        m_new = jnp.maximum(m_prev, jnp.max(top, axis=1, keepdims=True))
        alpha = jnp.exp(m_prev - m_new)
        ps = [jnp.exp(sb - m_new) for sb in slabs]
        psum = ps[0]
        for pc in ps[1:]:
            psum = psum + pc
        m_ref[...] = m_new
        l_ref[...] = alpha * l_ref[...] + psum
        p = jnp.concatenate(ps, axis=1).astype(BF16)
        acc_ref[...] = alpha * acc_ref[...] + jnp.dot(
            p, v_ref[0, pl.ds(off, width), :], preferred_element_type=F32)

    def wide(f, carry):
        block_step(pl.multiple_of(f * 2 * t, 2 * t), 2 * t, None)
        return carry

    lax.fori_loop(0, lax.shift_right_logical(qi, 1), wide, 0)

    @pl.when((qi & 1) == 1)
    def _():
        block_step(pl.multiple_of((qi - 1) * t, t), t, None)

    r = lax.broadcasted_iota(jnp.int32, (2 * t, t), 0)
    c = lax.broadcasted_iota(jnp.int32, (2 * t, t), 1)
    hidden = (c >> CHUNK_SHIFT) > ((r & (t - 1)) >> CHUNK_SHIFT)
    block_step(pl.multiple_of(qi * t, t), t, hidden)

    lam, lam_init = _lambda(lam_ref)
    l = jnp.sum(l_ref[...], axis=1, keepdims=True)
    o_ref[0] = _finish_head(acc_ref[...], l, t, lam, lam_init, subw_ref[...]).astype(BF16)


def _prompt_attention(q, kb, vb, lam5, subw):
    bsz, seq, width = q.shape
    t = ATTN_TILE
    q_spec = pl.BlockSpec((1, t, HEAD_W), lambda b, h, i: (b, i, h))
    kv_spec = pl.BlockSpec((1, seq, HEAD_W), lambda b, h, i: (b, 0, h))
    block_bytes = 2 * t * HEAD_W * 2 + 2 * seq * HEAD_W * 2
    scratch_bytes = 2 * t * HEAD_W * 2 + 3 * 2 * t * LANES * 4
    return pl.pallas_call(
        functools.partial(_prompt_attn_kernel, t=t),
        grid=(bsz, N_HEADS, seq // t),
        in_specs=[pl.BlockSpec((5, HEAD_DIM), lambda b, h, i: (0, 0)),
                  pl.BlockSpec((1, HEAD_W), lambda b, h, i: (0, 0)),
                  q_spec, kv_spec, kv_spec],
        out_specs=q_spec,
        out_shape=jax.ShapeDtypeStruct((bsz, seq, width), BF16),
        scratch_shapes=[pltpu.VMEM((2 * t, HEAD_W), BF16),
                        pltpu.VMEM((2 * t, LANES), F32),
                        pltpu.VMEM((2 * t, LANES), F32),
                        pltpu.VMEM((2 * t, HEAD_W), F32)],
        compiler_params=pltpu.CompilerParams(
            dimension_semantics=("arbitrary", "arbitrary", "arbitrary"),
            vmem_limit_bytes=_vmem_limit(block_bytes, scratch_bytes)),
        name="prompt_attention",
    )(lam5, subw, q, kb, vb)


def _sample_attn_kernel(layer_ref, lam_ref, subw_ref, q_ref, kn_ref, vn_ref, kc_ref, vc_ref, o_ref,
                        q2_ref, *, t):
    del layer_ref
    lam, lam_init = _lambda(lam_ref)
    subw = subw_ref[...]
    for h in range(N_HEADS):
        sl = slice(h * HEAD_W, (h + 1) * HEAD_W)
        _stack_maps(q_ref[0, :, sl], q2_ref)
        q2 = q2_ref[...]
        s_c = _scores(q2, kc_ref[0, 0, :, h, :].astype(BF16))
        s_n = _scores(q2, kn_ref[0, :, sl])
        m = jnp.maximum(jnp.max(s_c, axis=1, keepdims=True), jnp.max(s_n, axis=1, keepdims=True))
        p_c = jnp.exp(s_c - m)
        p_n = jnp.exp(s_n - m)
        l = jnp.sum(p_c, axis=1, keepdims=True) + jnp.sum(p_n, axis=1, keepdims=True)
        acc = (jnp.dot(p_c.astype(BF16), vc_ref[0, 0, :, h, :].astype(BF16), preferred_element_type=F32)
               + jnp.dot(p_n.astype(BF16), vn_ref[0, :, sl], preferred_element_type=F32))
        o_ref[0, :, sl] = _finish_head(acc, l, t, lam, lam_init, subw).astype(BF16)


def _sample_attention(layer, q, kb, vb, cache_k, cache_v, lam5, subw):
    bsz, t, width = q.shape
    past = cache_k.shape[2]
    new_spec = pl.BlockSpec((1, t, width), lambda b, layer_ref: (b, 0, 0))
    cache_spec = pl.BlockSpec((1, 1, past, N_HEADS, HEAD_W),
                              lambda b, layer_ref: (layer_ref[0], b, 0, 0, 0))
    block_bytes = 4 * t * width * 2 + 2 * past * width * 4
    return pl.pallas_call(
        functools.partial(_sample_attn_kernel, t=t),
        grid_spec=pltpu.PrefetchScalarGridSpec(
            num_scalar_prefetch=1,
            grid=(bsz,),
            in_specs=[pl.BlockSpec((5, HEAD_DIM), lambda b, layer_ref: (0, 0)),
                      pl.BlockSpec((1, HEAD_W), lambda b, layer_ref: (0, 0)),
                      new_spec, new_spec, new_spec, cache_spec, cache_spec],
            out_specs=new_spec,
            scratch_shapes=[pltpu.VMEM((2 * t, HEAD_W), BF16)]),
        out_shape=jax.ShapeDtypeStruct((bsz, t, width), BF16),
        compiler_params=pltpu.CompilerParams(
            dimension_semantics=("arbitrary",),
            vmem_limit_bytes=_vmem_limit(block_bytes)),
        name="sample_attention",
    )(jnp.full((1,), layer, jnp.int32), lam5, subw, q, kb, vb, cache_k, cache_v)


def _out_kernel(px_ref, hist_ref, pg_ref, o_ref, ag_ref, ga_ref, gb_ref, x_ref,
                wp_ref, ps_ref, wa_ref, wb_ref, wo_ref, g_ref, b_ref,
                y_ref, ext_ref, *, nb, tm, pos0, first_tile_has_no_history):
    i = pl.program_id(1)
    rows = nb * tm
    hist = hist_ref[...]
    if first_tile_has_no_history:
        hist = jnp.where(i == 0, 0.0, hist)
    ext_ref[:, 0:HIST_ROWS, :] = hist
    ext_ref[:, HIST_ROWS:HIST_ROWS + tm, :] = px_ref[...]

    pos = pos0 + i * tm + lax.broadcasted_iota(jnp.int32, (nb, tm, POOL_GROUP), 1)
    ya_parts = []
    for g, w in enumerate(POOL_WINDOWS):
        sl = slice(g * POOL_GROUP, (g + 1) * POOL_GROUP)
        u = ext_ref[:, HIST_ROWS:HIST_ROWS + tm, sl]
        s = u
        for j in range(1, w):
            s = s + ext_ref[:, HIST_ROWS - j:HIST_ROWS - j + tm, sl]
        cnt = jnp.minimum(pos + 1, w).astype(F32)
        pooled = (s / cnt - u).reshape(rows, POOL_GROUP)
        ya_parts.append(jnp.dot(pooled.astype(BF16), wp_ref[g], preferred_element_type=F32))
    ya = jnp.concatenate(ya_parts, axis=1) * ps_ref[...]
    ya = ya * _silu(pg_ref[...].reshape(rows, D_POOL).astype(F32))

    yb = o_ref[...].reshape(rows, D_MODEL).astype(F32) * _silu(ag_ref[...].reshape(rows, D_MODEL).astype(F32))

    ma = jnp.dot(ya.astype(BF16), wa_ref[...], preferred_element_type=F32)
    mb = jnp.dot(yb.astype(BF16), wb_ref[...], preferred_element_type=F32)
    merged = (_sigmoid(ga_ref[...].reshape(rows, D_MODEL).astype(F32)) * ma
              + _sigmoid(gb_ref[...].reshape(rows, D_MODEL).astype(F32)) * mb)
    out = jnp.dot(merged.astype(BF16), wo_ref[...], preferred_element_type=F32)
    z = ALPHA * x_ref[...].reshape(rows, D_MODEL) + out
    y_ref[...] = _layer_norm_rows(z, g_ref[...], b_ref[...]).reshape(nb, tm, D_MODEL)


def _out_projection(px, hist, pg, o, ag, ga, gb, x, wp, ps, wa, wb, wo, g, b, *,
                    nb, tm, pos0, first_tile_has_no_history):
    bsz, seq, _ = px.shape
    hist_blocks_per_tile = tm // HIST_ROWS

    def act_spec(width):
        return pl.BlockSpec((nb, tm, width), lambda bi, i: (bi, i, 0))

    def whole(shape):
        return pl.BlockSpec(shape, lambda bi, i: (0,) * len(shape))

    hist_spec = pl.BlockSpec(
        (nb, HIST_ROWS, D_POOL),
        lambda bi, i: (bi, jnp.maximum(i * hist_blocks_per_tile - 1, 0), 0))
    rows = nb * tm
    block_bytes = (rows * (D_POOL * 4 + D_POOL * 2 + 4 * D_MODEL * 2 + 2 * D_MODEL * 4)
                   + (D_POOL + 2 * D_MODEL) * D_MODEL * 2)
    return pl.pallas_call(
        functools.partial(_out_kernel, nb=nb, tm=tm, pos0=pos0,
                          first_tile_has_no_history=first_tile_has_no_history),
        grid=(bsz // nb, seq // tm),
        in_specs=[act_spec(D_POOL), hist_spec, act_spec(D_POOL), act_spec(D_MODEL),
                  act_spec(D_MODEL), act_spec(D_MODEL), act_spec(D_MODEL), act_spec(D_MODEL),
                  whole((len(POOL_WINDOWS), POOL_GROUP, POOL_GROUP)), whole((1, D_POOL)),
                  whole((D_POOL, D_MODEL)), whole((D_MODEL, D_MODEL)), whole((D_MODEL, D_MODEL)),
                  whole((1, D_MODEL)), whole((1, D_MODEL))],
        out_specs=act_spec(D_MODEL),
        out_shape=jax.ShapeDtypeStruct((bsz, seq, D_MODEL), F32),
        scratch_shapes=[pltpu.VMEM((nb, HIST_ROWS + tm, D_POOL), F32)],
        compiler_params=pltpu.CompilerParams(
            dimension_semantics=("arbitrary", "arbitrary"),
            vmem_limit_bytes=_vmem_limit(block_bytes)),
        name="out_projection",
    )(px, hist, pg, o, ag, ga, gb, x, wp, ps, wa, wb, wo, g, b)


def kernel(x_prompt, x_sample, cache_k, cache_v, state_pool, ln_in_g, ln_in_b, w_in, w_pool,
           pool_scale, lambda_qk, subln_w, w_a, w_b, w_o, ln_g, ln_b):
    bsz, seq, _ = x_prompt.shape
    dbsz, dseq, _ = x_sample.shape
    past = cache_k.shape[2]

    cos_p, sin_p = _rope_tables(seq, 0, seq)
    cos_s, sin_s = _rope_tables(dbsz * dseq, past, dseq)

    xp = _entry_layer_norm(x_prompt.reshape(bsz * seq, D_MODEL), ln_in_g, ln_in_b)
    xs = _entry_layer_norm(x_sample.reshape(dbsz * dseq, D_MODEL), ln_in_g, ln_in_b)

    state16 = jnp.pad(state_pool, ((0, 0), (0, 0), (HIST_ROWS - POOL_HIST, 0), (0, 0)))

    kp_all = jnp.zeros((DEPTH, bsz * seq * N_HEADS, HEAD_W), F32)
    vp_all = jnp.zeros((DEPTH, bsz * seq * N_HEADS, HEAD_W), F32)
    ks_all = jnp.zeros((DEPTH, dbsz * dseq * N_HEADS, HEAD_W), F32)
    vs_all = jnp.zeros((DEPTH, dbsz * dseq * N_HEADS, HEAD_W), F32)

    hp_l, hs_l = [], []
    for l in range(DEPTH):
        lam_init = 0.8 - 0.6 * math.exp(-0.3 * l)
        lam5 = jnp.concatenate([lambda_qk[l].astype(F32), jnp.full((1, HEAD_DIM), lam_init, F32)], axis=0)
        w_in_b = w_in[l].astype(BF16)
        wp_b, wa_b, wb_b, wo_b = (w_pool[l].astype(BF16), w_a[l].astype(BF16),
                                  w_b[l].astype(BF16), w_o[l].astype(BF16))
        ps = pool_scale[l].reshape(1, D_POOL)
        subw = subln_w[l].reshape(1, HEAD_W)
        g = ln_g[l].reshape(1, D_MODEL)
        b = ln_b[l].reshape(1, D_MODEL)

        px, pg, q, kb, vb, ag, ga, gb, kp_all, vp_all = _in_projection(
            l, xp, w_in_b, cos_p, sin_p, kp_all, vp_all)
        r3 = lambda a: a.reshape(bsz, seq, a.shape[-1])
        o = _prompt_attention(r3(q), r3(kb), r3(vb), lam5, subw)
        px3 = r3(px)
        xp = _out_projection(px3, px3, r3(pg), o, r3(ag), r3(ga), r3(gb), r3(xp),
                             wp_b, ps, wa_b, wb_b, wo_b, g, b,
                             nb=1, tm=ROW_TILE, pos0=0,
                             first_tile_has_no_history=True).reshape(bsz * seq, D_MODEL)
        hp_l.append(px3[:, seq - POOL_HIST:, :])

        px, pg, q, kb, vb, ag, ga, gb, ks_all, vs_all = _in_projection(
            l, xs, w_in_b, cos_s, sin_s, ks_all, vs_all)
        s3 = lambda a: a.reshape(dbsz, dseq, a.shape[-1])
        o = _sample_attention(l, s3(q), s3(kb), s3(vb), cache_k, cache_v, lam5, subw)
        px3 = s3(px)
        xs = _out_projection(px3, state16[l], s3(pg), o, s3(ag), s3(ga), s3(gb), s3(xs),
                             wp_b, ps, wa_b, wb_b, wo_b, g, b,
                             nb=dbsz, tm=dseq, pos0=past,
                             first_tile_has_no_history=False).reshape(dbsz * dseq, D_MODEL)
        hs_l.append(jnp.concatenate([state_pool[l], px3], axis=1)[:, -POOL_HIST:, :])

    return (xp.reshape(bsz, seq, D_MODEL), xs.reshape(dbsz, dseq, D_MODEL),
            kp_all.reshape(DEPTH, bsz, seq, N_HEADS, HEAD_W),
            vp_all.reshape(DEPTH, bsz, seq, N_HEADS, HEAD_W),
            jnp.stack(hp_l),
            ks_all.reshape(DEPTH, dbsz, dseq, N_HEADS, HEAD_W),
            vs_all.reshape(DEPTH, dbsz, dseq, N_HEADS, HEAD_W),
            jnp.stack(hs_l))
```

```python
import functools
import math

import jax
import jax.numpy as jnp
from jax import lax
from jax.experimental import pallas as pl
from jax.experimental.pallas import tpu as pltpu

F32 = jnp.float32
BF16 = jnp.bfloat16

D_MODEL = 1024
DEPTH = 4
CHUNK = 64
CHUNK_SHIFT = 6
D_POOL = 512
POOL_WINDOWS = (2, 4, 8, 16)
POOL_GROUP = 128
POOL_HIST = 15
HIST_ROWS = 16
N_HEADS = 8
HEAD_DIM = 64
HEAD_W = 2 * HEAD_DIM
ROPE_HALF = HEAD_DIM // 2
ROPE_THETA = 10000.0
QK_SCALE = HEAD_DIM ** -0.5
LN_EPS = 1e-5
RMS_EPS = 1e-5
ALPHA = (2 * DEPTH) ** 0.25
COL_PX, COL_PG, COL_Q, COL_K, COL_V, COL_AG, COL_GA, COL_GB = 0, 512, 1024, 2048, 3072, 4096, 5120, 6144
D_IN = 7168

V7X_VMEM_BYTES = 64 * 1024 * 1024
LANES = 128

MASKED = -1e30
ROW_TILE = 512
ATTN_TILE = 512
ATTN_HEADS_PER_STEP = 2
TEMP_VMEM_BYTES = 16 << 20


def _vmem_limit(pipelined_bytes, resident_bytes=0):
    return min(2 * pipelined_bytes + resident_bytes + TEMP_VMEM_BYTES, V7X_VMEM_BYTES - (4 << 20))


def _sigmoid(x):
    return 1.0 / (1.0 + jnp.exp(-x))


def _silu(x):
    return x * _sigmoid(x)


def _layer_norm_rows(z, g, b):
    mu = jnp.mean(z, axis=-1, keepdims=True)
    zc = z - mu
    var = jnp.mean(zc * zc, axis=-1, keepdims=True)
    return zc * lax.rsqrt(var + LN_EPS) * g + b


def _ln_kernel(x_ref, g_ref, b_ref, y_ref):
    y_ref[...] = _layer_norm_rows(x_ref[...], g_ref[...], b_ref[...])


def _entry_layer_norm(x2d, g, b):
    rows = x2d.shape[0]
    tm = min(ROW_TILE, rows)
    row_spec = pl.BlockSpec((tm, D_MODEL), lambda i: (i, 0))
    vec_spec = pl.BlockSpec((1, D_MODEL), lambda i: (0, 0))
    return pl.pallas_call(
        _ln_kernel,
        grid=(rows // tm,),
        in_specs=[row_spec, vec_spec, vec_spec],
        out_specs=row_spec,
        out_shape=jax.ShapeDtypeStruct((rows, D_MODEL), F32),
        compiler_params=pltpu.CompilerParams(
            dimension_semantics=("arbitrary",),
            vmem_limit_bytes=_vmem_limit(2 * tm * D_MODEL * 4)),
        name="entry_ln",
    )(x2d, g.reshape(1, D_MODEL), b.reshape(1, D_MODEL))


def _rope_table_kernel(cos_ref, sin_ref, *, pos0, period, tm):
    i = pl.program_id(0)
    lane = lax.broadcasted_iota(jnp.int32, (tm, LANES), 1)
    row = lax.broadcasted_iota(jnp.int32, (tm, LANES), 0) + i * tm
    freq = (lane & (ROPE_HALF - 1)).astype(F32)
    inv = jnp.exp(freq * (-math.log(ROPE_THETA) / ROPE_HALF))
    pos = (pos0 + (row & (period - 1))).astype(F32)
    ang = pos * inv
    sign = jnp.where((lane & ROPE_HALF) == 0, -1.0, 1.0)
    cos_ref[...] = jnp.cos(ang)
    sin_ref[...] = jnp.sin(ang) * sign


def _rope_tables(rows, pos0, period):
    tm = min(ROW_TILE, rows)
    spec = pl.BlockSpec((tm, LANES), lambda i: (i, 0))
    shape = jax.ShapeDtypeStruct((rows, LANES), F32)
    return pl.pallas_call(
        functools.partial(_rope_table_kernel, pos0=pos0, period=period, tm=tm),
        grid=(rows // tm,),
        out_specs=[spec, spec],
        out_shape=[shape, shape],
        compiler_params=pltpu.CompilerParams(dimension_semantics=("arbitrary",)),
        name="rope_tables",
    )()


def _inproj_kernel(layer_ref, x_ref, w_ref, cos_ref, sin_ref, k_all_in, v_all_in,
                   px_ref, pg_ref, q_ref, kb_ref, vb_ref, ag_ref, ga_ref, gb_ref, k_ref, v_ref,
                   xb_ref):
    del layer_ref, k_all_in, v_all_in
    tm = x_ref.shape[0]
    xb_ref[...] = x_ref[...].astype(BF16)

    def proj(col, width):
        return jnp.dot(xb_ref[...], w_ref[:, col:col + width], preferred_element_type=F32)

    def head_rows(h):
        return pl.ds(h, tm, stride=N_HEADS)

    px_ref[...] = proj(COL_PX, D_POOL)
    pg_ref[...] = proj(COL_PG, D_POOL).astype(BF16)

    cos = cos_ref[...]
    sin = sin_ref[...]
    lane = lax.broadcasted_iota(jnp.int32, (tm, LANES), 1)
    holds_x1 = (lane & ROPE_HALF) == 0

    def rope(y):
        partner = jnp.where(holds_x1,
                            pltpu.roll(y, LANES - ROPE_HALF, 1),
                            pltpu.roll(y, ROPE_HALF, 1))
        return y * cos + partner * sin

    q = proj(COL_Q, D_MODEL)
    for h in range(N_HEADS):
        sl = slice(h * HEAD_W, (h + 1) * HEAD_W)
        q_ref[:, sl] = (rope(q[:, sl]) * QK_SCALE).astype(BF16)
    k = proj(COL_K, D_MODEL)
    for h in range(N_HEADS):
        sl = slice(h * HEAD_W, (h + 1) * HEAD_W)
        kr = rope(k[:, sl])
        k_ref[0, head_rows(h), :] = kr
        kb_ref[:, sl] = kr.astype(BF16)
    v = proj(COL_V, D_MODEL)
    for h in range(N_HEADS):
        sl = slice(h * HEAD_W, (h + 1) * HEAD_W)
        v_ref[0, head_rows(h), :] = v[:, sl]
    vb_ref[...] = v.astype(BF16)
    ag_ref[...] = proj(COL_AG, D_MODEL).astype(BF16)
    ga_ref[...] = proj(COL_GA, D_MODEL).astype(BF16)
    gb_ref[...] = proj(COL_GB, D_MODEL).astype(BF16)


def _in_projection(layer, x2d, w_bf16, cos_t, sin_t, k_all, v_all):
    rows = x2d.shape[0]
    slab_shape = jax.ShapeDtypeStruct((DEPTH, rows * N_HEADS, HEAD_W), F32)
    tm = min(ROW_TILE, rows)
    table_blocks = cos_t.shape[0] // tm

    def rows_spec(width):
        return pl.BlockSpec((tm, width), lambda i, layer_ref: (i, 0))

    table_spec = pl.BlockSpec((tm, LANES), lambda i, layer_ref: (i % table_blocks, 0))
    w_spec = pl.BlockSpec((D_MODEL, D_IN), lambda i, layer_ref: (0, 0), pipeline_mode=pl.Buffered(1))
    kv_out_spec = pl.BlockSpec((1, tm * N_HEADS, HEAD_W), lambda i, layer_ref: (layer_ref[0], i, 0))
    any_spec = pl.BlockSpec(memory_space=pl.ANY)
    outs = [(D_POOL, F32), (D_POOL, BF16), (D_MODEL, BF16), (D_MODEL, BF16), (D_MODEL, BF16),
            (D_MODEL, BF16), (D_MODEL, BF16), (D_MODEL, BF16)]
    out_row_bytes = sum(wd * jnp.dtype(dt).itemsize for wd, dt in outs) + 2 * D_MODEL * 4
    block_bytes = tm * (D_MODEL * 4 + 2 * LANES * 4 + out_row_bytes)
    res = pl.pallas_call(
        _inproj_kernel,
        grid_spec=pltpu.PrefetchScalarGridSpec(
            num_scalar_prefetch=1,
            grid=(rows // tm,),
            in_specs=[rows_spec(D_MODEL), w_spec, table_spec, table_spec, any_spec, any_spec],
            out_specs=[rows_spec(wd) for wd, _ in outs] + [kv_out_spec, kv_out_spec],
            scratch_shapes=[pltpu.VMEM((tm, D_MODEL), BF16)]),
        out_shape=[jax.ShapeDtypeStruct((rows, wd), dt) for wd, dt in outs] + [slab_shape, slab_shape],
        input_output_aliases={5: 8, 6: 9},
        compiler_params=pltpu.CompilerParams(
            dimension_semantics=("arbitrary",),
            vmem_limit_bytes=_vmem_limit(block_bytes, D_MODEL * D_IN * 2 + tm * D_MODEL * 2)),
        name="in_projection",
    )(jnp.full((1,), layer, jnp.int32), x2d, w_bf16, cos_t, sin_t, k_all, v_all)
    return res


def _stack_maps(q, q2_ref):
    t = q.shape[0]
    lane = lax.broadcasted_iota(jnp.int32, q.shape, 1)
    zero = jnp.zeros_like(q)
    q2_ref[0:t, :] = jnp.where(lane < HEAD_DIM, q, zero)
    q2_ref[t:2 * t, :] = jnp.where(lane >= HEAD_DIM, q, zero)


def _scores(q2, k_blk):
    return lax.dot_general(q2, k_blk, (((1,), (1,)), ((), ())), preferred_element_type=F32)


def _lambda(lam_ref):
    lq = lam_ref[...]
    d1 = jnp.sum(lq[0:1, :] * lq[1:2, :], axis=1, keepdims=True)
    d2 = jnp.sum(lq[2:3, :] * lq[3:4, :], axis=1, keepdims=True)
    lam_init = lq[4:5, 0:1]
    return jnp.exp(d1) - jnp.exp(d2) + lam_init, lam_init


def _finish_head(acc, l, t, lam, lam_init, subw):
    o1 = acc[0:t, :] / l[0:t, :]
    o2 = acc[t:2 * t, :] / l[t:2 * t, :]
    o = o1 - lam * o2
    ms = jnp.mean(o * o, axis=1, keepdims=True)
    return o * lax.rsqrt(ms + RMS_EPS) * subw * (1.0 - lam_init)


def _prompt_attn_kernel(lam_ref, subw_ref, q_ref, k_ref, v_ref, o_ref,
                        q2_ref, m_ref, l_ref, acc_ref, *, t, heads):
    qi = pl.program_id(2)
    for g in range(heads):
        _stack_maps(q_ref[0, :, g * HEAD_W:(g + 1) * HEAD_W], q2_ref.at[g])
    m_ref[...] = jnp.full(m_ref.shape, MASKED, F32)
    l_ref[...] = jnp.zeros(l_ref.shape, F32)
    acc_ref[...] = jnp.zeros(acc_ref.shape, F32)

    row_chunk = (lax.broadcasted_iota(jnp.int32, (2 * t, LANES), 0) & (t - 1)) >> CHUNK_SHIFT
    lane_chunk = lax.broadcasted_iota(jnp.int32, (2 * t, LANES), 1) >> CHUNK_SHIFT
    chunks_per_slab = LANES // CHUNK

    def head_step(g, off, width, diagonal):
        lanes = slice(g * HEAD_W, (g + 1) * HEAD_W)
        s = _scores(q2_ref[g], k_ref[0, pl.ds(off, width), lanes])
        slabs = [s[:, cb * LANES:(cb + 1) * LANES] for cb in range(width // LANES)]
        if diagonal:
            first = (width - t) // LANES
            slabs = slabs[:first] + [
                jnp.where(lane_chunk + cb * chunks_per_slab > row_chunk, MASKED, sb)
                for cb, sb in enumerate(slabs[first:])]
        top = slabs[0]
        for sb in slabs[1:]:
            top = jnp.maximum(top, sb)
        m_prev = m_ref[g]
        m_new = jnp.maximum(m_prev, jnp.max(top, axis=1, keepdims=True))
        alpha = jnp.exp(m_prev - m_new)
        ps = [jnp.exp(sb - m_new) for sb in slabs]
        psum = ps[0]
        for pc in ps[1:]:
            psum = psum + pc
        m_ref[g] = m_new
        l_ref[g] = alpha * l_ref[g] + psum
        p = jnp.concatenate(ps, axis=1).astype(BF16)
        acc_ref[g] = alpha * acc_ref[g] + jnp.dot(
            p, v_ref[0, pl.ds(off, width), lanes], preferred_element_type=F32)

    def block_step(off, width, diagonal=False):
        for g in range(heads):
            head_step(g, off, width, diagonal)

    def wide(f, carry):
        block_step(pl.multiple_of(f * 2 * t, 2 * t), 2 * t)
        return carry

    lax.fori_loop(0, lax.shift_right_logical(qi, 1), wide, 0)

    @pl.when((qi & 1) == 1)
    def _():
        block_step(pl.multiple_of((qi - 1) * t, t), t)

    block_step(pl.multiple_of(qi * t, t), t, diagonal=True)

    lam, lam_init = _lambda(lam_ref)
    for g in range(heads):
        l = jnp.sum(l_ref[g], axis=1, keepdims=True)
        o_ref[0, :, g * HEAD_W:(g + 1) * HEAD_W] = _finish_head(
            acc_ref[g], l, t, lam, lam_init, subw_ref[...]).astype(BF16)


def _prompt_attention(q, kb, vb, lam5, subw):
    bsz, seq, width = q.shape
    t = ATTN_TILE
    heads = ATTN_HEADS_PER_STEP
    q_spec = pl.BlockSpec((1, t, heads * HEAD_W), lambda b, h, i: (b, i, h))
    kv_spec = pl.BlockSpec((1, seq, heads * HEAD_W), lambda b, h, i: (b, 0, h))
    block_bytes = heads * (2 * t * HEAD_W * 2 + 2 * seq * HEAD_W * 2)
    scratch_bytes = heads * (2 * t * HEAD_W * 2 + 3 * 2 * t * LANES * 4)
    return pl.pallas_call(
        functools.partial(_prompt_attn_kernel, t=t, heads=heads),
        grid=(bsz, N_HEADS // heads, seq // t),
        in_specs=[pl.BlockSpec((5, HEAD_DIM), lambda b, h, i: (0, 0)),
                  pl.BlockSpec((1, HEAD_W), lambda b, h, i: (0, 0)),
                  q_spec, kv_spec, kv_spec],
        out_specs=q_spec,
        out_shape=jax.ShapeDtypeStruct((bsz, seq, width), BF16),
        scratch_shapes=[pltpu.VMEM((heads, 2 * t, HEAD_W), BF16),
                        pltpu.VMEM((heads, 2 * t, LANES), F32),
                        pltpu.VMEM((heads, 2 * t, LANES), F32),
                        pltpu.VMEM((heads, 2 * t, HEAD_W), F32)],
        compiler_params=pltpu.CompilerParams(
            dimension_semantics=("arbitrary", "arbitrary", "arbitrary"),
            vmem_limit_bytes=_vmem_limit(block_bytes, scratch_bytes)),
        name="prompt_attention",
    )(lam5, subw, q, kb, vb)


def _sample_attn_kernel(layer_ref, lam_ref, subw_ref, q_ref, kn_ref, vn_ref, kc_ref, vc_ref, o_ref,
                        q2_ref, *, t):
    del layer_ref
    rows_per_head = 2 * t
    rows = N_HEADS * rows_per_head
    for h in range(N_HEADS):
        _stack_maps(q_ref[0, :, h * HEAD_W:(h + 1) * HEAD_W],
                    q2_ref.at[h * rows_per_head:(h + 1) * rows_per_head])
    q2 = q2_ref[...]
    s_c = _scores(q2, kc_ref[0, 0].astype(BF16))
    s_n = _scores(q2, kn_ref[0].astype(BF16))

    r = lax.broadcasted_iota(jnp.int32, (rows, LANES), 0)
    c = lax.broadcasted_iota(jnp.int32, (rows, LANES), 1)
    head_shift = rows_per_head.bit_length() - 1
    other_head = (c & (N_HEADS - 1)) != (r >> head_shift)

    def slabs_of(s):
        return [jnp.where(other_head, MASKED, s[:, cb * LANES:(cb + 1) * LANES])
                for cb in range(s.shape[1] // LANES)]

    slabs_c = slabs_of(s_c)
    slabs_n = slabs_of(s_n)
    top = slabs_n[0]
    for sb in slabs_n[1:] + slabs_c:
        top = jnp.maximum(top, sb)
    m = jnp.max(top, axis=1, keepdims=True)
    ps_c = [jnp.exp(sb - m) for sb in slabs_c]
    ps_n = [jnp.exp(sb - m) for sb in slabs_n]
    psum = ps_n[0]
    for pc in ps_n[1:] + ps_c:
        psum = psum + pc
    l_all = jnp.sum(psum, axis=1, keepdims=True)
    acc_all = (jnp.dot(jnp.concatenate(ps_c, axis=1).astype(BF16), vc_ref[0, 0].astype(BF16),
                       preferred_element_type=F32)
               + jnp.dot(jnp.concatenate(ps_n, axis=1).astype(BF16), vn_ref[0].astype(BF16),
                         preferred_element_type=F32))

    lam, lam_init = _lambda(lam_ref)
    subw = subw_ref[...]
    for h in range(N_HEADS):
        hr = slice(h * rows_per_head, (h + 1) * rows_per_head)
        o_ref[0, :, h * HEAD_W:(h + 1) * HEAD_W] = _finish_head(
            acc_all[hr, :], l_all[hr, :], t, lam, lam_init, subw).astype(BF16)


def _sample_attention(layer, q, k_new, v_new, cache_k, cache_v, lam5, subw):
    bsz, t, width = q.shape
    past = cache_k.shape[2]
    q_spec = pl.BlockSpec((1, t, width), lambda b, layer_ref: (b, 0, 0))
    new_spec = pl.BlockSpec((1, t * N_HEADS, HEAD_W), lambda b, layer_ref: (layer_ref[0], b, 0))
    cache_spec = pl.BlockSpec((1, 1, past * N_HEADS, HEAD_W),
                              lambda b, layer_ref: (layer_ref[0], b, 0, 0))
    cache_k2 = cache_k.reshape(DEPTH, bsz, past * N_HEADS, HEAD_W)
    cache_v2 = cache_v.reshape(DEPTH, bsz, past * N_HEADS, HEAD_W)
    rows = 2 * t * N_HEADS
    block_bytes = 2 * t * width * 2 + 2 * t * width * 4 + 2 * past * width * 4
    temp_bytes = rows * past * N_HEADS * (4 + 2) + 2 * past * width * 2
    return pl.pallas_call(
        functools.partial(_sample_attn_kernel, t=t),
        grid_spec=pltpu.PrefetchScalarGridSpec(
            num_scalar_prefetch=1,
            grid=(bsz,),
            in_specs=[pl.BlockSpec((5, HEAD_DIM), lambda b, layer_ref: (0, 0)),
                      pl.BlockSpec((1, HEAD_W), lambda b, layer_ref: (0, 0)),
                      q_spec, new_spec, new_spec, cache_spec, cache_spec],
            out_specs=q_spec,
            scratch_shapes=[pltpu.VMEM((rows, HEAD_W), BF16)]),
        out_shape=jax.ShapeDtypeStruct((bsz, t, width), BF16),
        compiler_params=pltpu.CompilerParams(
            dimension_semantics=("arbitrary",),
            vmem_limit_bytes=_vmem_limit(block_bytes, temp_bytes)),
        name="sample_attention",
    )(jnp.full((1,), layer, jnp.int32), lam5, subw, q, k_new, v_new, cache_k2, cache_v2)


def _out_kernel(px_ref, hist_ref, pg_ref, o_ref, ag_ref, ga_ref, gb_ref, x_ref,
                wp_ref, ps_ref, wa_ref, wb_ref, wo_ref, g_ref, b_ref,
                y_ref, ext_ref, *, nb, tm, pos0, first_tile_has_no_history):
    i = pl.program_id(1)
    rows = nb * tm
    hist = hist_ref[...]
    if first_tile_has_no_history:
        hist = jnp.where(i == 0, 0.0, hist)
    ext_ref[:, 0:HIST_ROWS, :] = hist
    ext_ref[:, HIST_ROWS:HIST_ROWS + tm, :] = px_ref[...]

    pos = pos0 + i * tm + lax.broadcasted_iota(jnp.int32, (nb, tm, POOL_GROUP), 1)
    ya_parts = []
    for g, w in enumerate(POOL_WINDOWS):
        sl = slice(g * POOL_GROUP, (g + 1) * POOL_GROUP)
        u = ext_ref[:, HIST_ROWS:HIST_ROWS + tm, sl]
        s = u
        for j in range(1, w):
            s = s + ext_ref[:, HIST_ROWS - j:HIST_ROWS - j + tm, sl]
        cnt = jnp.minimum(pos + 1, w).astype(F32)
        pooled = (s / cnt - u).reshape(rows, POOL_GROUP)
        ya_parts.append(jnp.dot(pooled.astype(BF16), wp_ref[g], preferred_element_type=F32))
    ya = (jnp.concatenate(ya_parts, axis=1) * ps_ref[...]).astype(BF16)
    ya = ya * _silu(pg_ref[...].reshape(rows, D_POOL))
    yb = o_ref[...].reshape(rows, D_MODEL) * _silu(ag_ref[...].reshape(rows, D_MODEL))

    ma = jnp.dot(ya, wa_ref[...], preferred_element_type=F32)
    mb = jnp.dot(yb, wb_ref[...], preferred_element_type=F32)
    merged = (_sigmoid(ga_ref[...].reshape(rows, D_MODEL)) * ma.astype(BF16)
              + _sigmoid(gb_ref[...].reshape(rows, D_MODEL)) * mb.astype(BF16))
    out = jnp.dot(merged, wo_ref[...], preferred_element_type=F32)
    z = ALPHA * x_ref[...].reshape(rows, D_MODEL) + out
    y_ref[...] = _layer_norm_rows(z, g_ref[...], b_ref[...]).reshape(nb, tm, D_MODEL)


def _out_projection(px, hist, pg, o, ag, ga, gb, x, wp, ps, wa, wb, wo, g, b, *,
                    nb, tm, pos0, first_tile_has_no_history):
    bsz, seq, _ = px.shape
    hist_blocks_per_tile = tm // HIST_ROWS

    def act_spec(width):
        return pl.BlockSpec((nb, tm, width), lambda bi, i: (bi, i, 0))

    def whole(shape):
        return pl.BlockSpec(shape, lambda bi, i: (0,) * len(shape))

    hist_spec = pl.BlockSpec(
        (nb, HIST_ROWS, D_POOL),
        lambda bi, i: (bi, jnp.maximum(i * hist_blocks_per_tile - 1, 0), 0))
    rows = nb * tm
    block_bytes = (rows * (D_POOL * 4 + D_POOL * 2 + 4 * D_MODEL * 2 + 2 * D_MODEL * 4)
                   + (D_POOL + 2 * D_MODEL) * D_MODEL * 2)
    return pl.pallas_call(
        functools.partial(_out_kernel, nb=nb, tm=tm, pos0=pos0,
                          first_tile_has_no_history=first_tile_has_no_history),
        grid=(bsz // nb, seq // tm),
        in_specs=[act_spec(D_POOL), hist_spec, act_spec(D_POOL), act_spec(D_MODEL),
                  act_spec(D_MODEL), act_spec(D_MODEL), act_spec(D_MODEL), act_spec(D_MODEL),
                  whole((len(POOL_WINDOWS), POOL_GROUP, POOL_GROUP)), whole((1, D_POOL)),
                  whole((D_POOL, D_MODEL)), whole((D_MODEL, D_MODEL)), whole((D_MODEL, D_MODEL)),
                  whole((1, D_MODEL)), whole((1, D_MODEL))],
        out_specs=act_spec(D_MODEL),
        out_shape=jax.ShapeDtypeStruct((bsz, seq, D_MODEL), F32),
        scratch_shapes=[pltpu.VMEM((nb, HIST_ROWS + tm, D_POOL), F32)],
        compiler_params=pltpu.CompilerParams(
            dimension_semantics=("arbitrary", "arbitrary"),
            vmem_limit_bytes=_vmem_limit(block_bytes)),
        name="out_projection",
    )(px, hist, pg, o, ag, ga, gb, x, wp, ps, wa, wb, wo, g, b)


def kernel(x_prompt, x_sample, cache_k, cache_v, state_pool, ln_in_g, ln_in_b, w_in, w_pool,
           pool_scale, lambda_qk, subln_w, w_a, w_b, w_o, ln_g, ln_b):
    bsz, seq, _ = x_prompt.shape
    dbsz, dseq, _ = x_sample.shape
    past = cache_k.shape[2]

    cos_p, sin_p = _rope_tables(seq, 0, seq)
    cos_s, sin_s = _rope_tables(dbsz * dseq, past, dseq)

    xp = _entry_layer_norm(x_prompt.reshape(bsz * seq, D_MODEL), ln_in_g, ln_in_b)
    xs = _entry_layer_norm(x_sample.reshape(dbsz * dseq, D_MODEL), ln_in_g, ln_in_b)

    state16 = jnp.pad(state_pool, ((0, 0), (0, 0), (HIST_ROWS - POOL_HIST, 0), (0, 0)))

    kp_all = jnp.zeros((DEPTH, bsz * seq * N_HEADS, HEAD_W), F32)
    vp_all = jnp.zeros((DEPTH, bsz * seq * N_HEADS, HEAD_W), F32)
    ks_all = jnp.zeros((DEPTH, dbsz * dseq * N_HEADS, HEAD_W), F32)
    vs_all = jnp.zeros((DEPTH, dbsz * dseq * N_HEADS, HEAD_W), F32)

    hp_l, hs_l = [], []
    for l in range(DEPTH):
        lam_init = 0.8 - 0.6 * math.exp(-0.3 * l)
        lam5 = jnp.concatenate([lambda_qk[l].astype(F32), jnp.full((1, HEAD_DIM), lam_init, F32)], axis=0)
        w_in_b = w_in[l].astype(BF16)
        wp_b, wa_b, wb_b, wo_b = (w_pool[l].astype(BF16), w_a[l].astype(BF16),
                                  w_b[l].astype(BF16), w_o[l].astype(BF16))
        ps = pool_scale[l].reshape(1, D_POOL)
        subw = subln_w[l].reshape(1, HEAD_W)
        g = ln_g[l].reshape(1, D_MODEL)
        b = ln_b[l].reshape(1, D_MODEL)

        px, pg, q, kb, vb, ag, ga, gb, kp_all, vp_all = _in_projection(
            l, xp, w_in_b, cos_p, sin_p, kp_all, vp_all)
        r3 = lambda a: a.reshape(bsz, seq, a.shape[-1])
        o = _prompt_attention(r3(q), r3(kb), r3(vb), lam5, subw)
        px3 = r3(px)
        xp = _out_projection(px3, px3, r3(pg), o, r3(ag), r3(ga), r3(gb), r3(xp),
                             wp_b, ps, wa_b, wb_b, wo_b, g, b,
                             nb=1, tm=ROW_TILE, pos0=0,
                             first_tile_has_no_history=True).reshape(bsz * seq, D_MODEL)
        hp_l.append(px3[:, seq - POOL_HIST:, :])

        px, pg, q, kb, vb, ag, ga, gb, ks_all, vs_all = _in_projection(
            l, xs, w_in_b, cos_s, sin_s, ks_all, vs_all)
        s3 = lambda a: a.reshape(dbsz, dseq, a.shape[-1])
        o = _sample_attention(l, s3(q), ks_all, vs_all, cache_k, cache_v, lam5, subw)
        px3 = s3(px)
        xs = _out_projection(px3, state16[l], s3(pg), o, s3(ag), s3(ga), s3(gb), s3(xs),
                             wp_b, ps, wa_b, wb_b, wo_b, g, b,
                             nb=dbsz, tm=dseq, pos0=past,
                             first_tile_has_no_history=False).reshape(dbsz * dseq, D_MODEL)
        hs_l.append(jnp.concatenate([state_pool[l], px3], axis=1)[:, -POOL_HIST:, :])

    return (xp.reshape(bsz, seq, D_MODEL), xs.reshape(dbsz, dseq, D_MODEL),
            kp_all.reshape(DEPTH, bsz, seq, N_HEADS, HEAD_W),
            vp_all.reshape(DEPTH, bsz, seq, N_HEADS, HEAD_W),
            jnp.stack(hp_l),
            ks_all.reshape(DEPTH, dbsz, dseq, N_HEADS, HEAD_W),
            vs_all.reshape(DEPTH, dbsz, dseq, N_HEADS, HEAD_W),
            jnp.stack(hs_l))
```

```python
import functools
import math

import jax
import jax.numpy as jnp
from jax import lax
from jax.experimental import pallas as pl
from jax.experimental.pallas import tpu as pltpu

F32 = jnp.float32
BF16 = jnp.bfloat16

D_MODEL = 1024
DEPTH = 4
CHUNK = 64
CHUNK_SHIFT = 6
D_POOL = 512
POOL_WINDOWS = (2, 4, 8, 16)
POOL_GROUP = 128
POOL_HIST = 15
HIST_ROWS = 16
N_HEADS = 8
HEAD_DIM = 64
HEAD_W = 2 * HEAD_DIM
ROPE_HALF = HEAD_DIM // 2
ROPE_THETA = 10000.0
QK_SCALE = HEAD_DIM ** -0.5
Q_PRESCALE = QK_SCALE * math.log2(math.e)
LN_EPS = 1e-5
RMS_EPS = 1e-5
ALPHA = (2 * DEPTH) ** 0.25
COL_PX, COL_PG, COL_Q, COL_K, COL_V, COL_AG, COL_GA, COL_GB = 0, 512, 1024, 2048, 3072, 4096, 5120, 6144
D_IN = 7168

V7X_VMEM_BYTES = 64 * 1024 * 1024
LANES = 128

MASKED = -1e30
ROW_TILE = 512
ATTN_TILE = 512
ATTN_HEADS_PER_STEP = 4
TEMP_VMEM_BYTES = 16 << 20


def _vmem_limit(pipelined_bytes, resident_bytes=0):
    return min(2 * pipelined_bytes + resident_bytes + TEMP_VMEM_BYTES, V7X_VMEM_BYTES - (4 << 20))


def _sigmoid(x):
    return 1.0 / (1.0 + jnp.exp(-x))


def _silu(x):
    return x * _sigmoid(x)


def _layer_norm_rows(z, g, b):
    mu = jnp.mean(z, axis=-1, keepdims=True)
    zc = z - mu
    var = jnp.mean(zc * zc, axis=-1, keepdims=True)
    return zc * lax.rsqrt(var + LN_EPS) * g + b


def _ln_kernel(x_ref, g_ref, b_ref, y_ref):
    y_ref[...] = _layer_norm_rows(x_ref[...], g_ref[...], b_ref[...])


def _entry_layer_norm(x2d, g, b):
    rows = x2d.shape[0]
    tm = min(ROW_TILE, rows)
    row_spec = pl.BlockSpec((tm, D_MODEL), lambda i: (i, 0))
    vec_spec = pl.BlockSpec((1, D_MODEL), lambda i: (0, 0))
    return pl.pallas_call(
        _ln_kernel,
        grid=(rows // tm,),
        in_specs=[row_spec, vec_spec, vec_spec],
        out_specs=row_spec,
        out_shape=jax.ShapeDtypeStruct((rows, D_MODEL), F32),
        compiler_params=pltpu.CompilerParams(
            dimension_semantics=("arbitrary",),
            vmem_limit_bytes=_vmem_limit(2 * tm * D_MODEL * 4)),
        name="entry_ln",
    )(x2d, g.reshape(1, D_MODEL), b.reshape(1, D_MODEL))


def _rope_table_kernel(cos_ref, sin_ref, *, pos0, period, tm):
    i = pl.program_id(0)
    lane = lax.broadcasted_iota(jnp.int32, (tm, LANES), 1)
    row = lax.broadcasted_iota(jnp.int32, (tm, LANES), 0) + i * tm
    freq = (lane & (ROPE_HALF - 1)).astype(F32)
    inv = jnp.exp(freq * (-math.log(ROPE_THETA) / ROPE_HALF))
    pos = (pos0 + (row & (period - 1))).astype(F32)
    ang = pos * inv
    sign = jnp.where((lane & ROPE_HALF) == 0, -1.0, 1.0)
    cos_ref[...] = jnp.cos(ang)
    sin_ref[...] = jnp.sin(ang) * sign


def _rope_tables(rows, pos0, period):
    tm = min(ROW_TILE, rows)
    spec = pl.BlockSpec((tm, LANES), lambda i: (i, 0))
    shape = jax.ShapeDtypeStruct((rows, LANES), F32)
    return pl.pallas_call(
        functools.partial(_rope_table_kernel, pos0=pos0, period=period, tm=tm),
        grid=(rows // tm,),
        out_specs=[spec, spec],
        out_shape=[shape, shape],
        compiler_params=pltpu.CompilerParams(dimension_semantics=("arbitrary",)),
        name="rope_tables",
    )()


def _inproj_kernel(layer_ref, x_ref, w_ref, cos_ref, sin_ref, k_all_in, v_all_in,
                   px_ref, pg_ref, q_ref, kb_ref, vb_ref, ag_ref, ga_ref, gb_ref, k_ref, v_ref,
                   xb_ref):
    del layer_ref, k_all_in, v_all_in
    tm = x_ref.shape[0]
    xb_ref[...] = x_ref[...].astype(BF16)

    def proj(col, width):
        return jnp.dot(xb_ref[...], w_ref[:, col:col + width], preferred_element_type=F32)

    def head_rows(h):
        return pl.ds(h, tm, stride=N_HEADS)

    px_ref[...] = proj(COL_PX, D_POOL)
    pg_ref[...] = proj(COL_PG, D_POOL).astype(BF16)

    cos = cos_ref[...]
    sin = sin_ref[...]
    lane = lax.broadcasted_iota(jnp.int32, (tm, LANES), 1)
    holds_x1 = (lane & ROPE_HALF) == 0

    def rope(y):
        partner = jnp.where(holds_x1,
                            pltpu.roll(y, LANES - ROPE_HALF, 1),
                            pltpu.roll(y, ROPE_HALF, 1))
        return y * cos + partner * sin

    q = proj(COL_Q, D_MODEL)
    for h in range(N_HEADS):
        sl = slice(h * HEAD_W, (h + 1) * HEAD_W)
        q_ref[:, sl] = (rope(q[:, sl]) * Q_PRESCALE).astype(BF16)
    k = proj(COL_K, D_MODEL)
    for h in range(N_HEADS):
        sl = slice(h * HEAD_W, (h + 1) * HEAD_W)
        kr = rope(k[:, sl])
        k_ref[0, head_rows(h), :] = kr
        kb_ref[:, sl] = kr.astype(BF16)
    v = proj(COL_V, D_MODEL)
    for h in range(N_HEADS):
        sl = slice(h * HEAD_W, (h + 1) * HEAD_W)
        v_ref[0, head_rows(h), :] = v[:, sl]
    vb_ref[...] = v.astype(BF16)
    ag_ref[...] = proj(COL_AG, D_MODEL).astype(BF16)
    ga_ref[...] = proj(COL_GA, D_MODEL).astype(BF16)
    gb_ref[...] = proj(COL_GB, D_MODEL).astype(BF16)


def _in_projection(layer, x2d, w_bf16, cos_t, sin_t, k_all, v_all):
    rows = x2d.shape[0]
    slab_shape = jax.ShapeDtypeStruct((DEPTH, rows * N_HEADS, HEAD_W), F32)
    tm = min(ROW_TILE, rows)
    table_blocks = cos_t.shape[0] // tm

    def rows_spec(width):
        return pl.BlockSpec((tm, width), lambda i, layer_ref: (i, 0))

    table_spec = pl.BlockSpec((tm, LANES), lambda i, layer_ref: (i % table_blocks, 0))
    w_spec = pl.BlockSpec((D_MODEL, D_IN), lambda i, layer_ref: (0, 0), pipeline_mode=pl.Buffered(1))
    kv_out_spec = pl.BlockSpec((1, tm * N_HEADS, HEAD_W), lambda i, layer_ref: (layer_ref[0], i, 0))
    any_spec = pl.BlockSpec(memory_space=pl.ANY)
    outs = [(D_POOL, F32), (D_POOL, BF16), (D_MODEL, BF16), (D_MODEL, BF16), (D_MODEL, BF16),
            (D_MODEL, BF16), (D_MODEL, BF16), (D_MODEL, BF16)]
    out_row_bytes = sum(wd * jnp.dtype(dt).itemsize for wd, dt in outs) + 2 * D_MODEL * 4
    block_bytes = tm * (D_MODEL * 4 + 2 * LANES * 4 + out_row_bytes)
    res = pl.pallas_call(
        _inproj_kernel,
        grid_spec=pltpu.PrefetchScalarGridSpec(
            num_scalar_prefetch=1,
            grid=(rows // tm,),
            in_specs=[rows_spec(D_MODEL), w_spec, table_spec, table_spec, any_spec, any_spec],
            out_specs=[rows_spec(wd) for wd, _ in outs] + [kv_out_spec, kv_out_spec],
            scratch_shapes=[pltpu.VMEM((tm, D_MODEL), BF16)]),
        out_shape=[jax.ShapeDtypeStruct((rows, wd), dt) for wd, dt in outs] + [slab_shape, slab_shape],
        input_output_aliases={5: 8, 6: 9},
        compiler_params=pltpu.CompilerParams(
            dimension_semantics=("arbitrary",),
            vmem_limit_bytes=_vmem_limit(block_bytes, D_MODEL * D_IN * 2 + tm * D_MODEL * 2)),
        name="in_projection",
    )(jnp.full((1,), layer, jnp.int32), x2d, w_bf16, cos_t, sin_t, k_all, v_all)
    return res


def _stack_maps(q, q2_ref):
    t = q.shape[0]
    lane = lax.broadcasted_iota(jnp.int32, q.shape, 1)
    zero = jnp.zeros_like(q)
    q2_ref[0:t, :] = jnp.where(lane < HEAD_DIM, q, zero)
    q2_ref[t:2 * t, :] = jnp.where(lane >= HEAD_DIM, q, zero)


def _scores(q2, k_blk):
    return lax.dot_general(q2, k_blk, (((1,), (1,)), ((), ())), preferred_element_type=F32)


def _lambda(lam_ref):
    lq = lam_ref[...]
    d1 = jnp.sum(lq[0:1, :] * lq[1:2, :], axis=1, keepdims=True)
    d2 = jnp.sum(lq[2:3, :] * lq[3:4, :], axis=1, keepdims=True)
    lam_init = lq[4:5, 0:1]
    return jnp.exp(d1) - jnp.exp(d2) + lam_init, lam_init


def _finish_head(acc, l, t, lam, lam_init, subw):
    o1 = acc[0:t, :] / l[0:t, :]
    o2 = acc[t:2 * t, :] / l[t:2 * t, :]
    o = o1 - lam * o2
    ms = jnp.mean(o * o, axis=1, keepdims=True)
    return o * lax.rsqrt(ms + RMS_EPS) * subw * (1.0 - lam_init)


def _prompt_attn_kernel(lam_ref, subw_ref, q_ref, k_ref, v_ref, o_ref,
                        q2_ref, vx_ref, m_ref, acc_ref, *, t, heads):
    qi = pl.program_id(2)
    hf = t // 2

    @pl.when(qi == 0)
    def _():
        for g in range(heads):
            vx_ref[g, :, 0:HEAD_W] = v_ref[0, :, g * HEAD_W:(g + 1) * HEAD_W]
            vx_ref[g, :, HEAD_W:2 * HEAD_W] = jnp.ones((vx_ref.shape[1], HEAD_W), BF16)

    for g in range(heads):
        lanes = slice(g * HEAD_W, (g + 1) * HEAD_W)
        _stack_maps(q_ref[0, 0:hf, lanes], q2_ref.at[g, 0:t])
        _stack_maps(q_ref[0, hf:t, lanes], q2_ref.at[g, t:2 * t])
    m_ref[...] = jnp.full(m_ref.shape, MASKED, F32)
    acc_ref[...] = jnp.zeros(acc_ref.shape, F32)

    half_row_chunk = (lax.broadcasted_iota(jnp.int32, (t, LANES), 0) & (hf - 1)) >> CHUNK_SHIFT
    lane_chunk = lax.broadcasted_iota(jnp.int32, (t, LANES), 1) >> CHUNK_SHIFT
    chunks_per_slab = LANES // CHUNK
    chunks_per_half = hf // CHUNK

    def head_step(g, off, width, row0=0, nrows=2 * t, masked_from=None):
        lanes = slice(g * HEAD_W, (g + 1) * HEAD_W)
        rows = slice(row0, row0 + nrows)
        s = _scores(q2_ref[g, rows], k_ref[0, pl.ds(off, width), lanes])
        slabs = [s[:, cb * LANES:(cb + 1) * LANES] for cb in range(width // LANES)]
        if masked_from is not None:
            row_chunk = half_row_chunk + (row0 // t) * chunks_per_half
            slabs = slabs[:masked_from] + [
                jnp.where(lane_chunk + cb * chunks_per_slab > row_chunk, MASKED, slabs[cb])
                for cb in range(masked_from, len(slabs))]
        top = slabs[0]
        for sb in slabs[1:]:
            top = jnp.maximum(top, sb)
        m_prev = m_ref[g, rows]
        m_new = jnp.maximum(m_prev, jnp.max(top, axis=1, keepdims=True))
        alpha = jnp.exp2(m_prev - m_new)
        p = jnp.concatenate([jnp.exp2(sb - m_new) for sb in slabs], axis=1).astype(BF16)
        m_ref[g, rows] = m_new
        pv = jnp.dot(p, vx_ref[g, pl.ds(off, width), :], preferred_element_type=F32)
        for part in range(2):
            cols = slice(part * HEAD_W, (part + 1) * HEAD_W)
            acc_ref[g, rows, cols] = alpha * acc_ref[g, rows, cols] + pv[:, cols]

    def block_step(off, width):
        for g in range(heads):
            head_step(g, off, width)

    def wide(f, carry):
        block_step(pl.multiple_of(f * 4 * t, 4 * t), 2 * t)
        block_step(pl.multiple_of(f * 4 * t + 2 * t, 2 * t), 2 * t)
        return carry

    lax.fori_loop(0, lax.shift_right_logical(qi, 2), wide, 0)

    @pl.when((qi & 2) == 2)
    def _():
        block_step(pl.multiple_of(lax.shift_right_logical(qi, 2) * 4 * t, 2 * t), 2 * t)

    @pl.when((qi & 1) == 1)
    def _():
        block_step(pl.multiple_of((qi - 1) * t, t), t)

    own = pl.multiple_of(qi * t, t)
    slabs_per_half = hf // LANES
    for g in range(heads):
        head_step(g, own, hf, row0=0, nrows=t, masked_from=0)
        head_step(g, own, t, row0=t, nrows=t, masked_from=slabs_per_half)

    lam, lam_init = _lambda(lam_ref)
    for g in range(heads):
        for half in range(2):
            rows = slice(half * t, (half + 1) * t)
            o_ref[0, half * hf:(half + 1) * hf, g * HEAD_W:(g + 1) * HEAD_W] = _finish_head(
                acc_ref[g, rows, 0:HEAD_W], acc_ref[g, rows, HEAD_W:2 * HEAD_W],
                hf, lam, lam_init, subw_ref[...]).astype(BF16)


def _prompt_attention(q, kb, vb, lam5, subw):
    bsz, seq, width = q.shape
    t = ATTN_TILE
    heads = ATTN_HEADS_PER_STEP
    q_spec = pl.BlockSpec((1, t, heads * HEAD_W), lambda b, h, i: (b, i, h))
    kv_spec = pl.BlockSpec((1, seq, heads * HEAD_W), lambda b, h, i: (b, 0, h))
    block_bytes = heads * (2 * t * HEAD_W * 2 + 2 * seq * HEAD_W * 2)
    scratch_bytes = heads * (2 * t * HEAD_W * 2 + seq * 2 * HEAD_W * 2 + 2 * t * (LANES + 2 * HEAD_W) * 4)
    return pl.pallas_call(
        functools.partial(_prompt_attn_kernel, t=t, heads=heads),
        grid=(bsz, N_HEADS // heads, seq // t),
        in_specs=[pl.BlockSpec((5, HEAD_DIM), lambda b, h, i: (0, 0)),
                  pl.BlockSpec((1, HEAD_W), lambda b, h, i: (0, 0)),
                  q_spec, kv_spec, kv_spec],
        out_specs=q_spec,
        out_shape=jax.ShapeDtypeStruct((bsz, seq, width), BF16),
        scratch_shapes=[pltpu.VMEM((heads, 2 * t, HEAD_W), BF16),
                        pltpu.VMEM((heads, seq, 2 * HEAD_W), BF16),
                        pltpu.VMEM((heads, 2 * t, LANES), F32),
                        pltpu.VMEM((heads, 2 * t, 2 * HEAD_W), F32)],
        compiler_params=pltpu.CompilerParams(
            dimension_semantics=("arbitrary", "arbitrary", "arbitrary"),
            vmem_limit_bytes=_vmem_limit(block_bytes, scratch_bytes)),
        name="prompt_attention",
    )(lam5, subw, q, kb, vb)


def _sample_attn_kernel(layer_ref, lam_ref, subw_ref, q_ref, kn_ref, vn_ref, kc_ref, vc_ref, o_ref,
                        q2_ref, *, t):
    del layer_ref
    rows_per_head = 2 * t
    rows = N_HEADS * rows_per_head
    for h in range(N_HEADS):
        _stack_maps(q_ref[0, :, h * HEAD_W:(h + 1) * HEAD_W],
                    q2_ref.at[h * rows_per_head:(h + 1) * rows_per_head])
    q2 = q2_ref[...]
    s_c = _scores(q2, kc_ref[0, 0].astype(BF16))
    s_n = _scores(q2, kn_ref[0].astype(BF16))

    r = lax.broadcasted_iota(jnp.int32, (rows, LANES), 0)
    c = lax.broadcasted_iota(jnp.int32, (rows, LANES), 1)
    head_shift = rows_per_head.bit_length() - 1
    other_head = (c & (N_HEADS - 1)) != (r >> head_shift)

    def slabs_of(s):
        return [jnp.where(other_head, MASKED, s[:, cb * LANES:(cb + 1) * LANES])
                for cb in range(s.shape[1] // LANES)]

    slabs_c = slabs_of(s_c)
    slabs_n = slabs_of(s_n)
    top = slabs_n[0]
    for sb in slabs_n[1:] + slabs_c:
        top = jnp.maximum(top, sb)
    m = jnp.max(top, axis=1, keepdims=True)
    ps_c = [jnp.exp2(sb - m) for sb in slabs_c]
    ps_n = [jnp.exp2(sb - m) for sb in slabs_n]
    psum = ps_n[0]
    for pc in ps_n[1:] + ps_c:
        psum = psum + pc
    l_all = jnp.sum(psum, axis=1, keepdims=True)
    acc_all = (jnp.dot(jnp.concatenate(ps_c, axis=1).astype(BF16), vc_ref[0, 0].astype(BF16),
                       preferred_element_type=F32)
               + jnp.dot(jnp.concatenate(ps_n, axis=1).astype(BF16), vn_ref[0].astype(BF16),
                         preferred_element_type=F32))

    lam, lam_init = _lambda(lam_ref)
    subw = subw_ref[...]
    for h in range(N_HEADS):
        hr = slice(h * rows_per_head, (h + 1) * rows_per_head)
        o_ref[0, :, h * HEAD_W:(h + 1) * HEAD_W] = _finish_head(
            acc_all[hr, :], l_all[hr, :], t, lam, lam_init, subw).astype(BF16)


def _sample_attention(layer, q, k_new, v_new, cache_k, cache_v, lam5, subw):
    bsz, t, width = q.shape
    past = cache_k.shape[2]
    q_spec = pl.BlockSpec((1, t, width), lambda b, layer_ref: (b, 0, 0))
    new_spec = pl.BlockSpec((1, t * N_HEADS, HEAD_W), lambda b, layer_ref: (layer_ref[0], b, 0))
    cache_spec = pl.BlockSpec((1, 1, past * N_HEADS, HEAD_W),
                              lambda b, layer_ref: (layer_ref[0], b, 0, 0))
    cache_k2 = cache_k.reshape(DEPTH, bsz, past * N_HEADS, HEAD_W)
    cache_v2 = cache_v.reshape(DEPTH, bsz, past * N_HEADS, HEAD_W)
    rows = 2 * t * N_HEADS
    block_bytes = 2 * t * width * 2 + 2 * t * width * 4 + 2 * past * width * 4
    temp_bytes = rows * past * N_HEADS * (4 + 2) + 2 * past * width * 2
    return pl.pallas_call(
        functools.partial(_sample_attn_kernel, t=t),
        grid_spec=pltpu.PrefetchScalarGridSpec(
            num_scalar_prefetch=1,
            grid=(bsz,),
            in_specs=[pl.BlockSpec((5, HEAD_DIM), lambda b, layer_ref: (0, 0)),
                      pl.BlockSpec((1, HEAD_W), lambda b, layer_ref: (0, 0)),
                      q_spec, new_spec, new_spec, cache_spec, cache_spec],
            out_specs=q_spec,
            scratch_shapes=[pltpu.VMEM((rows, HEAD_W), BF16)]),
        out_shape=jax.ShapeDtypeStruct((bsz, t, width), BF16),
        compiler_params=pltpu.CompilerParams(
            dimension_semantics=("arbitrary",),
            vmem_limit_bytes=_vmem_limit(block_bytes, temp_bytes)),
        name="sample_attention",
    )(jnp.full((1,), layer, jnp.int32), lam5, subw, q, k_new, v_new, cache_k2, cache_v2)


def _out_kernel(px_ref, hist_ref, pg_ref, o_ref, ag_ref, ga_ref, gb_ref, x_ref,
                wp_ref, ps_ref, wa_ref, wb_ref, wo_ref, g_ref, b_ref,
                y_ref, ext_ref, *, nb, tm, pos0, first_tile_has_no_history):
    i = pl.program_id(1)
    rows = nb * tm
    hist = hist_ref[...]
    if first_tile_has_no_history:
        hist = jnp.where(i == 0, 0.0, hist)
    ext_ref[:, 0:HIST_ROWS, :] = hist
    ext_ref[:, HIST_ROWS:HIST_ROWS + tm, :] = px_ref[...]

    pos = pos0 + i * tm + lax.broadcasted_iota(jnp.int32, (nb, tm, POOL_GROUP), 1)
    ya_parts = []
    for g, w in enumerate(POOL_WINDOWS):
        sl = slice(g * POOL_GROUP, (g + 1) * POOL_GROUP)
        u = ext_ref[:, HIST_ROWS:HIST_ROWS + tm, sl]
        s = u
        for j in range(1, w):
            s = s + ext_ref[:, HIST_ROWS - j:HIST_ROWS - j + tm, sl]
        cnt = jnp.minimum(pos + 1, w).astype(F32)
        pooled = (s / cnt - u).reshape(rows, POOL_GROUP)
        ya_parts.append(jnp.dot(pooled.astype(BF16), wp_ref[g], preferred_element_type=F32))
    ya = (jnp.concatenate(ya_parts, axis=1) * ps_ref[...]).astype(BF16)
    ya = ya * _silu(pg_ref[...].reshape(rows, D_POOL))
    yb = o_ref[...].reshape(rows, D_MODEL) * _silu(ag_ref[...].reshape(rows, D_MODEL))

    ma = jnp.dot(ya, wa_ref[...], preferred_element_type=F32)
    mb = jnp.dot(yb, wb_ref[...], preferred_element_type=F32)
    merged = (_sigmoid(ga_ref[...].reshape(rows, D_MODEL)) * ma.astype(BF16)
              + _sigmoid(gb_ref[...].reshape(rows, D_MODEL)) * mb.astype(BF16))
    out = jnp.dot(merged, wo_ref[...], preferred_element_type=F32)
    z = ALPHA * x_ref[...].reshape(rows, D_MODEL) + out
    y_ref[...] = _layer_norm_rows(z, g_ref[...], b_ref[...]).reshape(nb, tm, D_MODEL)


def _out_projection(px, hist, pg, o, ag, ga, gb, x, wp, ps, wa, wb, wo, g, b, *,
                    nb, tm, pos0, first_tile_has_no_history):
    bsz, seq, _ = px.shape
    hist_blocks_per_tile = tm // HIST_ROWS

    def act_spec(width):
        return pl.BlockSpec((nb, tm, width), lambda bi, i: (bi, i, 0))

    def whole(shape):
        return pl.BlockSpec(shape, lambda bi, i: (0,) * len(shape))

    hist_spec = pl.BlockSpec(
        (nb, HIST_ROWS, D_POOL),
        lambda bi, i: (bi, jnp.maximum(i * hist_blocks_per_tile - 1, 0), 0))
    rows = nb * tm
    block_bytes = (rows * (D_POOL * 4 + D_POOL * 2 + 4 * D_MODEL * 2 + 2 * D_MODEL * 4)
                   + (D_POOL + 2 * D_MODEL) * D_MODEL * 2)
    return pl.pallas_call(
        functools.partial(_out_kernel, nb=nb, tm=tm, pos0=pos0,
                          first_tile_has_no_history=first_tile_has_no_history),
        grid=(bsz // nb, seq // tm),
        in_specs=[act_spec(D_POOL), hist_spec, act_spec(D_POOL), act_spec(D_MODEL),
                  act_spec(D_MODEL), act_spec(D_MODEL), act_spec(D_MODEL), act_spec(D_MODEL),
                  whole((len(POOL_WINDOWS), POOL_GROUP, POOL_GROUP)), whole((1, D_POOL)),
                  whole((D_POOL, D_MODEL)), whole((D_MODEL, D_MODEL)), whole((D_MODEL, D_MODEL)),
                  whole((1, D_MODEL)), whole((1, D_MODEL))],
        out_specs=act_spec(D_MODEL),
        out_shape=jax.ShapeDtypeStruct((bsz, seq, D_MODEL), F32),
        scratch_shapes=[pltpu.VMEM((nb, HIST_ROWS + tm, D_POOL), F32)],
        compiler_params=pltpu.CompilerParams(
            dimension_semantics=("arbitrary", "arbitrary"),
            vmem_limit_bytes=_vmem_limit(block_bytes)),
        name="out_projection",
    )(px, hist, pg, o, ag, ga, gb, x, wp, ps, wa, wb, wo, g, b)


def kernel(x_prompt, x_sample, cache_k, cache_v, state_pool, ln_in_g, ln_in_b, w_in, w_pool,
           pool_scale, lambda_qk, subln_w, w_a, w_b, w_o, ln_g, ln_b):
    bsz, seq, _ = x_prompt.shape
    dbsz, dseq, _ = x_sample.shape
    past = cache_k.shape[2]

    cos_p, sin_p = _rope_tables(seq, 0, seq)
    cos_s, sin_s = _rope_tables(dbsz * dseq, past, dseq)

    xp = _entry_layer_norm(x_prompt.reshape(bsz * seq, D_MODEL), ln_in_g, ln_in_b)
    xs = _entry_layer_norm(x_sample.reshape(dbsz * dseq, D_MODEL), ln_in_g, ln_in_b)

    state16 = jnp.pad(state_pool, ((0, 0), (0, 0), (HIST_ROWS - POOL_HIST, 0), (0, 0)))

    kp_all = jnp.zeros((DEPTH, bsz * seq * N_HEADS, HEAD_W), F32)
    vp_all = jnp.zeros((DEPTH, bsz * seq * N_HEADS, HEAD_W), F32)
    ks_all = jnp.zeros((DEPTH, dbsz * dseq * N_HEADS, HEAD_W), F32)
    vs_all = jnp.zeros((DEPTH, dbsz * dseq * N_HEADS, HEAD_W), F32)

    hp_l, hs_l = [], []
    for l in range(DEPTH):
        lam_init = 0.8 - 0.6 * math.exp(-0.3 * l)
        lam5 = jnp.concatenate([lambda_qk[l].astype(F32), jnp.full((1, HEAD_DIM), lam_init, F32)], axis=0)
        w_in_b = w_in[l].astype(BF16)
        wp_b, wa_b, wb_b, wo_b = (w_pool[l].astype(BF16), w_a[l].astype(BF16),
                                  w_b[l].astype(BF16), w_o[l].astype(BF16))
        ps = pool_scale[l].reshape(1, D_POOL)
        subw = subln_w[l].reshape(1, HEAD_W)
        g = ln_g[l].reshape(1, D_MODEL)
        b = ln_b[l].reshape(1, D_MODEL)

        px, pg, q, kb, vb, ag, ga, gb, kp_all, vp_all = _in_projection(
            l, xp, w_in_b, cos_p, sin_p, kp_all, vp_all)
        r3 = lambda a: a.reshape(bsz, seq, a.shape[-1])
        o = _prompt_attention(r3(q), r3(kb), r3(vb), lam5, subw)
        px3 = r3(px)
        xp = _out_projection(px3, px3, r3(pg), o, r3(ag), r3(ga), r3(gb), r3(xp),
                             wp_b, ps, wa_b, wb_b, wo_b, g, b,
                             nb=1, tm=ROW_TILE, pos0=0,
                             first_tile_has_no_history=True).reshape(bsz * seq, D_MODEL)
        hp_l.append(px3[:, seq - POOL_HIST:, :])

        px, pg, q, kb, vb, ag, ga, gb, ks_all, vs_all = _in_projection(
            l, xs, w_in_b, cos_s, sin_s, ks_all, vs_all)
        s3 = lambda a: a.reshape(dbsz, dseq, a.shape[-1])
        o = _sample_attention(l, s3(q), ks_all, vs_all, cache_k, cache_v, lam5, subw)
        px3 = s3(px)
        xs = _out_projection(px3, state16[l], s3(pg), o, s3(ag), s3(ga), s3(gb), s3(xs),
                             wp_b, ps, wa_b, wb_b, wo_b, g, b,
                             nb=dbsz, tm=dseq, pos0=past,
                             first_tile_has_no_history=False).reshape(dbsz * dseq, D_MODEL)
        hs_l.append(jnp.concatenate([state_pool[l], px3], axis=1)[:, -POOL_HIST:, :])

    return (xp.reshape(bsz, seq, D_MODEL), xs.reshape(dbsz, dseq, D_MODEL),
            kp_all.reshape(DEPTH, bsz, seq, N_HEADS, HEAD_W),
            vp_all.reshape(DEPTH, bsz, seq, N_HEADS, HEAD_W),
            jnp.stack(hp_l),
            ks_all.reshape(DEPTH, dbsz, dseq, N_HEADS, HEAD_W),
            vs_all.reshape(DEPTH, dbsz, dseq, N_HEADS, HEAD_W),
            jnp.stack(hs_l))
```

```python
import functools
import math

import jax
import jax.numpy as jnp
from jax import lax
from jax.experimental import pallas as pl
from jax.experimental.pallas import tpu as pltpu

F32 = jnp.float32
BF16 = jnp.bfloat16

D_MODEL = 1024
DEPTH = 4
CHUNK = 64
CHUNK_SHIFT = 6
D_POOL = 512
POOL_WINDOWS = (2, 4, 8, 16)
POOL_GROUP = 128
POOL_HIST = 15
HIST_ROWS = 32
N_HEADS = 8
HEAD_DIM = 64
HEAD_W = 2 * HEAD_DIM
ROPE_HALF = HEAD_DIM // 2
ROPE_THETA = 10000.0
QK_SCALE = HEAD_DIM ** -0.5
Q_PRESCALE = QK_SCALE * math.log2(math.e)
LN_EPS = 1e-5
RMS_EPS = 1e-5
ALPHA = (2 * DEPTH) ** 0.25
COL_PX, COL_PG, COL_Q, COL_K, COL_V, COL_AG, COL_GA, COL_GB = 0, 512, 1024, 2048, 3072, 4096, 5120, 6144
D_IN = 7168

V7X_VMEM_BYTES = 64 * 1024 * 1024
LANES = 128

MASKED = -1e30
ROW_TILE = 512
ATTN_TILE = 512
ATTN_WIDE_KEYS = 1024
ATTN_HEADS_PER_STEP = 4
TEMP_VMEM_BYTES = 16 << 20


def _vmem_limit(pipelined_bytes, resident_bytes=0):
    return min(2 * pipelined_bytes + resident_bytes + TEMP_VMEM_BYTES, V7X_VMEM_BYTES - (4 << 20))


def _sigmoid(x):
    return 1.0 / (1.0 + jnp.exp(-x))


def _silu(x):
    return x * _sigmoid(x)


def _layer_norm_rows(z, g, b):
    mu = jnp.mean(z, axis=-1, keepdims=True)
    zc = z - mu
    var = jnp.mean(zc * zc, axis=-1, keepdims=True)
    return zc * lax.rsqrt(var + LN_EPS) * g + b


def _ln_kernel(x_ref, g_ref, b_ref, y_ref):
    y_ref[...] = _layer_norm_rows(x_ref[...], g_ref[...], b_ref[...])


def _entry_layer_norm(x2d, g, b):
    rows = x2d.shape[0]
    tm = min(ROW_TILE, rows)
    row_spec = pl.BlockSpec((tm, D_MODEL), lambda i: (i, 0))
    vec_spec = pl.BlockSpec((1, D_MODEL), lambda i: (0, 0))
    return pl.pallas_call(
        _ln_kernel,
        grid=(rows // tm,),
        in_specs=[row_spec, vec_spec, vec_spec],
        out_specs=row_spec,
        out_shape=jax.ShapeDtypeStruct((rows, D_MODEL), F32),
        compiler_params=pltpu.CompilerParams(
            dimension_semantics=("arbitrary",),
            vmem_limit_bytes=_vmem_limit(2 * tm * D_MODEL * 4)),
        name="entry_ln",
    )(x2d, g.reshape(1, D_MODEL), b.reshape(1, D_MODEL))


def _rope_table_kernel(cos_ref, sin_ref, *, pos0, period, tm):
    i = pl.program_id(0)
    lane = lax.broadcasted_iota(jnp.int32, (tm, LANES), 1)
    row = lax.broadcasted_iota(jnp.int32, (tm, LANES), 0) + i * tm
    freq = (lane & (ROPE_HALF - 1)).astype(F32)
    inv = jnp.exp(freq * (-math.log(ROPE_THETA) / ROPE_HALF))
    pos = (pos0 + (row & (period - 1))).astype(F32)
    ang = pos * inv
    sign = jnp.where((lane & ROPE_HALF) == 0, -1.0, 1.0)
    cos_ref[...] = jnp.cos(ang)
    sin_ref[...] = jnp.sin(ang) * sign


def _rope_tables(rows, pos0, period):
    tm = min(ROW_TILE, rows)
    spec = pl.BlockSpec((tm, LANES), lambda i: (i, 0))
    shape = jax.ShapeDtypeStruct((rows, LANES), F32)
    return pl.pallas_call(
        functools.partial(_rope_table_kernel, pos0=pos0, period=period, tm=tm),
        grid=(rows // tm,),
        out_specs=[spec, spec],
        out_shape=[shape, shape],
        compiler_params=pltpu.CompilerParams(dimension_semantics=("arbitrary",)),
        name="rope_tables",
    )()


def _inproj_kernel(layer_ref, x_ref, w_ref, cos_ref, sin_ref, k_all_in, v_all_in,
                   px_ref, pg_ref, q_ref, kb_ref, vb_ref, ag_ref, ga_ref, gb_ref, k_ref, v_ref,
                   xb_ref):
    del layer_ref, k_all_in, v_all_in
    tm = x_ref.shape[0]
    xb_ref[...] = x_ref[...].astype(BF16)

    def proj(col, width):
        return jnp.dot(xb_ref[...], w_ref[:, col:col + width], preferred_element_type=F32)

    def head_rows(h):
        return pl.ds(h, tm, stride=N_HEADS)

    px_ref[...] = proj(COL_PX, D_POOL)
    pg_ref[...] = proj(COL_PG, D_POOL).astype(BF16)

    cos = cos_ref[...]
    sin = sin_ref[...]
    lane = lax.broadcasted_iota(jnp.int32, (tm, LANES), 1)
    holds_x1 = (lane & ROPE_HALF) == 0

    def rope(y):
        partner = jnp.where(holds_x1,
                            pltpu.roll(y, LANES - ROPE_HALF, 1),
                            pltpu.roll(y, ROPE_HALF, 1))
        return y * cos + partner * sin

    q = proj(COL_Q, D_MODEL)
    for h in range(N_HEADS):
        sl = slice(h * HEAD_W, (h + 1) * HEAD_W)
        q_ref[:, sl] = (rope(q[:, sl]) * Q_PRESCALE).astype(BF16)
    k = proj(COL_K, D_MODEL)
    for h in range(N_HEADS):
        sl = slice(h * HEAD_W, (h + 1) * HEAD_W)
        kr = rope(k[:, sl])
        k_ref[0, head_rows(h), :] = kr
        kb_ref[:, sl] = kr.astype(BF16)
    v = proj(COL_V, D_MODEL)
    for h in range(N_HEADS):
        sl = slice(h * HEAD_W, (h + 1) * HEAD_W)
        v_ref[0, head_rows(h), :] = v[:, sl]
    vb_ref[...] = v.astype(BF16)
    ag_ref[...] = proj(COL_AG, D_MODEL).astype(BF16)
    ga_ref[...] = proj(COL_GA, D_MODEL).astype(BF16)
    gb_ref[...] = proj(COL_GB, D_MODEL).astype(BF16)


def _in_projection(layer, x2d, w_bf16, cos_t, sin_t, k_all, v_all):
    rows = x2d.shape[0]
    slab_shape = jax.ShapeDtypeStruct((DEPTH, rows * N_HEADS, HEAD_W), F32)
    tm = min(ROW_TILE, rows)
    table_blocks = cos_t.shape[0] // tm

    def rows_spec(width):
        return pl.BlockSpec((tm, width), lambda i, layer_ref: (i, 0))

    table_spec = pl.BlockSpec((tm, LANES), lambda i, layer_ref: (i % table_blocks, 0))
    w_spec = pl.BlockSpec((D_MODEL, D_IN), lambda i, layer_ref: (0, 0), pipeline_mode=pl.Buffered(1))
    kv_out_spec = pl.BlockSpec((1, tm * N_HEADS, HEAD_W), lambda i, layer_ref: (layer_ref[0], i, 0))
    any_spec = pl.BlockSpec(memory_space=pl.ANY)
    outs = [(D_POOL, F32), (D_POOL, BF16), (D_MODEL, BF16), (D_MODEL, BF16), (D_MODEL, BF16),
            (D_MODEL, BF16), (D_MODEL, BF16), (D_MODEL, BF16)]
    out_row_bytes = sum(wd * jnp.dtype(dt).itemsize for wd, dt in outs) + 2 * D_MODEL * 4
    block_bytes = tm * (D_MODEL * 4 + 2 * LANES * 4 + out_row_bytes)
    res = pl.pallas_call(
        _inproj_kernel,
        grid_spec=pltpu.PrefetchScalarGridSpec(
            num_scalar_prefetch=1,
            grid=(rows // tm,),
            in_specs=[rows_spec(D_MODEL), w_spec, table_spec, table_spec, any_spec, any_spec],
            out_specs=[rows_spec(wd) for wd, _ in outs] + [kv_out_spec, kv_out_spec],
            scratch_shapes=[pltpu.VMEM((tm, D_MODEL), BF16)]),
        out_shape=[jax.ShapeDtypeStruct((rows, wd), dt) for wd, dt in outs] + [slab_shape, slab_shape],
        input_output_aliases={5: 8, 6: 9},
        compiler_params=pltpu.CompilerParams(
            dimension_semantics=("arbitrary",),
            vmem_limit_bytes=_vmem_limit(block_bytes, D_MODEL * D_IN * 2 + tm * D_MODEL * 2)),
        name="in_projection",
    )(jnp.full((1,), layer, jnp.int32), x2d, w_bf16, cos_t, sin_t, k_all, v_all)
    return res


def _stack_maps(q, q2_ref):
    t = q.shape[0]
    lane = lax.broadcasted_iota(jnp.int32, q.shape, 1)
    zero = jnp.zeros_like(q)
    q2_ref[0:t, :] = jnp.where(lane < HEAD_DIM, q, zero)
    q2_ref[t:2 * t, :] = jnp.where(lane >= HEAD_DIM, q, zero)


def _scores(q2, k_blk):
    return lax.dot_general(q2, k_blk, (((1,), (1,)), ((), ())), preferred_element_type=F32)


def _lambda(lam_ref):
    lq = lam_ref[...]
    d1 = jnp.sum(lq[0:1, :] * lq[1:2, :], axis=1, keepdims=True)
    d2 = jnp.sum(lq[2:3, :] * lq[3:4, :], axis=1, keepdims=True)
    lam_init = lq[4:5, 0:1]
    return jnp.exp(d1) - jnp.exp(d2) + lam_init, lam_init


def _finish_head(acc, l, t, lam, lam_init, subw):
    o1 = acc[0:t, :] / l[0:t, :]
    o2 = acc[t:2 * t, :] / l[t:2 * t, :]
    o = o1 - lam * o2
    ms = jnp.mean(o * o, axis=1, keepdims=True)
    return o * lax.rsqrt(ms + RMS_EPS) * subw * (1.0 - lam_init)


def _prompt_attn_kernel(lam_ref, subw_ref, q_ref, k_ref, v_ref, o_ref,
                        q2_ref, vx_ref, m_ref, acc_ref, *, t, heads):
    qi = pl.program_id(2)
    hf = t // 2

    @pl.when(qi == 0)
    def _():
        for g in range(heads):
            vx_ref[g, :, 0:HEAD_W] = v_ref[0, :, g * HEAD_W:(g + 1) * HEAD_W]
            vx_ref[g, :, HEAD_W:2 * HEAD_W] = jnp.ones((vx_ref.shape[1], HEAD_W), BF16)

    for g in range(heads):
        lanes = slice(g * HEAD_W, (g + 1) * HEAD_W)
        _stack_maps(q_ref[0, 0:hf, lanes], q2_ref.at[g, 0:t])
        _stack_maps(q_ref[0, hf:t, lanes], q2_ref.at[g, t:2 * t])
    m_ref[...] = jnp.full(m_ref.shape, MASKED, F32)
    acc_ref[...] = jnp.zeros(acc_ref.shape, F32)

    half_row_chunk = (lax.broadcasted_iota(jnp.int32, (t, LANES), 0) & (hf - 1)) >> CHUNK_SHIFT
    lane_chunk = lax.broadcasted_iota(jnp.int32, (t, LANES), 1) >> CHUNK_SHIFT
    chunks_per_slab = LANES // CHUNK
    own_bias = [jnp.where(lane_chunk + cb * chunks_per_slab > half_row_chunk, MASKED, 0.0)
                for cb in range(hf // LANES)]

    def head_step(g, off, width, row0=0, nrows=2 * t, masked_from=None):
        lanes = slice(g * HEAD_W, (g + 1) * HEAD_W)
        rows = slice(row0, row0 + nrows)
        s = _scores(q2_ref[g, rows], k_ref[0, pl.ds(off, width), lanes])
        slabs = [s[:, cb * LANES:(cb + 1) * LANES] for cb in range(width // LANES)]
        if masked_from is not None:
            slabs = slabs[:masked_from] + [
                slabs[cb] + own_bias[cb - masked_from] for cb in range(masked_from, len(slabs))]
        top = slabs[0]
        for sb in slabs[1:]:
            top = jnp.maximum(top, sb)
        m_prev = m_ref[g, rows]
        m_new = jnp.maximum(m_prev, jnp.max(top, axis=1, keepdims=True))
        alpha = jnp.exp2(m_prev - m_new)
        p = jnp.concatenate([jnp.exp2(sb - m_new) for sb in slabs], axis=1).astype(BF16)
        m_ref[g, rows] = m_new
        pv = jnp.dot(p, vx_ref[g, pl.ds(off, width), :], preferred_element_type=F32)
        for part in range(2):
            cols = slice(part * HEAD_W, (part + 1) * HEAD_W)
            acc_ref[g, rows, cols] = alpha * acc_ref[g, rows, cols] + pv[:, cols]

    def block_step(off, width):
        for g in range(heads):
            head_step(g, off, width)

    ww = ATTN_WIDE_KEYS
    n_wide = lax.shift_right_logical(qi * t, ww.bit_length() - 1)

    def wide_pair(f, carry):
        block_step(pl.multiple_of(f * 2 * ww, 2 * ww), ww)
        block_step(pl.multiple_of(f * 2 * ww + ww, ww), ww)
        return carry

    lax.fori_loop(0, lax.shift_right_logical(n_wide, 1), wide_pair, 0)

    @pl.when((n_wide & 1) == 1)
    def _():
        block_step(pl.multiple_of((n_wide - 1) * ww, ww), ww)

    if t < ww:
        def narrow(f, carry):
            block_step(pl.multiple_of(n_wide * ww + f * t, t), t)
            return carry

        lax.fori_loop(0, qi - n_wide * (ww // t), narrow, 0)

    own = pl.multiple_of(qi * t, t)
    slabs_per_half = hf // LANES
    for g in range(heads):
        head_step(g, own, hf, row0=0, nrows=t, masked_from=0)
        head_step(g, own, t, row0=t, nrows=t, masked_from=slabs_per_half)

    lam, lam_init = _lambda(lam_ref)
    for g in range(heads):
        for half in range(2):
            rows = slice(half * t, (half + 1) * t)
            o_ref[0, half * hf:(half + 1) * hf, g * HEAD_W:(g + 1) * HEAD_W] = _finish_head(
                acc_ref[g, rows, 0:HEAD_W], acc_ref[g, rows, HEAD_W:2 * HEAD_W],
                hf, lam, lam_init, subw_ref[...]).astype(BF16)


def _prompt_attention(q, kb, vb, lam5, subw):
    bsz, seq, width = q.shape
    t = ATTN_TILE
    heads = ATTN_HEADS_PER_STEP
    q_spec = pl.BlockSpec((1, t, heads * HEAD_W), lambda b, h, i: (b, i, h))
    kv_spec = pl.BlockSpec((1, seq, heads * HEAD_W), lambda b, h, i: (b, 0, h))
    block_bytes = heads * (2 * t * HEAD_W * 2 + 2 * seq * HEAD_W * 2)
    scratch_bytes = heads * (2 * t * HEAD_W * 2 + seq * 2 * HEAD_W * 2 + 2 * t * (LANES + 2 * HEAD_W) * 4)
    return pl.pallas_call(
        functools.partial(_prompt_attn_kernel, t=t, heads=heads),
        grid=(bsz, N_HEADS // heads, seq // t),
        in_specs=[pl.BlockSpec((5, HEAD_DIM), lambda b, h, i: (0, 0)),
                  pl.BlockSpec((1, HEAD_W), lambda b, h, i: (0, 0)),
                  q_spec, kv_spec, kv_spec],
        out_specs=q_spec,
        out_shape=jax.ShapeDtypeStruct((bsz, seq, width), BF16),
        scratch_shapes=[pltpu.VMEM((heads, 2 * t, HEAD_W), BF16),
                        pltpu.VMEM((heads, seq, 2 * HEAD_W), BF16),
                        pltpu.VMEM((heads, 2 * t, LANES), F32),
                        pltpu.VMEM((heads, 2 * t, 2 * HEAD_W), F32)],
        compiler_params=pltpu.CompilerParams(
            dimension_semantics=("arbitrary", "arbitrary", "arbitrary"),
            vmem_limit_bytes=_vmem_limit(block_bytes, scratch_bytes)),
        name="prompt_attention",
    )(lam5, subw, q, kb, vb)


def _sample_attn_kernel(layer_ref, lam_ref, subw_ref, q_ref, kn_ref, vn_ref, kc_ref, vc_ref, o_ref,
                        q2_ref, *, t):
    del layer_ref
    rows_per_head = 2 * t
    rows = N_HEADS * rows_per_head
    for h in range(N_HEADS):
        _stack_maps(q_ref[0, :, h * HEAD_W:(h + 1) * HEAD_W],
                    q2_ref.at[h * rows_per_head:(h + 1) * rows_per_head])
    q2 = q2_ref[...]
    s_c = _scores(q2, kc_ref[0, 0].astype(BF16))
    s_n = _scores(q2, kn_ref[0].astype(BF16))

    r = lax.broadcasted_iota(jnp.int32, (rows, LANES), 0)
    c = lax.broadcasted_iota(jnp.int32, (rows, LANES), 1)
    head_shift = rows_per_head.bit_length() - 1
    other_head = (c & (N_HEADS - 1)) != (r >> head_shift)

    def slabs_of(s):
        return [jnp.where(other_head, MASKED, s[:, cb * LANES:(cb + 1) * LANES])
                for cb in range(s.shape[1] // LANES)]

    slabs_c = slabs_of(s_c)
    slabs_n = slabs_of(s_n)
    top = slabs_n[0]
    for sb in slabs_n[1:] + slabs_c:
        top = jnp.maximum(top, sb)
    m = jnp.max(top, axis=1, keepdims=True)
    ps_c = [jnp.exp2(sb - m) for sb in slabs_c]
    ps_n = [jnp.exp2(sb - m) for sb in slabs_n]
    psum = ps_n[0]
    for pc in ps_n[1:] + ps_c:
        psum = psum + pc
    l_all = jnp.sum(psum, axis=1, keepdims=True)
    acc_all = (jnp.dot(jnp.concatenate(ps_c, axis=1).astype(BF16), vc_ref[0, 0].astype(BF16),
                       preferred_element_type=F32)
               + jnp.dot(jnp.concatenate(ps_n, axis=1).astype(BF16), vn_ref[0].astype(BF16),
                         preferred_element_type=F32))

    lam, lam_init = _lambda(lam_ref)
    subw = subw_ref[...]
    for h in range(N_HEADS):
        hr = slice(h * rows_per_head, (h + 1) * rows_per_head)
        o_ref[0, :, h * HEAD_W:(h + 1) * HEAD_W] = _finish_head(
            acc_all[hr, :], l_all[hr, :], t, lam, lam_init, subw).astype(BF16)


def _sample_attention(layer, q, k_new, v_new, cache_k, cache_v, lam5, subw):
    bsz, t, width = q.shape
    past = cache_k.shape[2]
    q_spec = pl.BlockSpec((1, t, width), lambda b, layer_ref: (b, 0, 0))
    new_spec = pl.BlockSpec((1, t * N_HEADS, HEAD_W), lambda b, layer_ref: (layer_ref[0], b, 0))
    cache_spec = pl.BlockSpec((1, 1, past * N_HEADS, HEAD_W),
                              lambda b, layer_ref: (layer_ref[0], b, 0, 0))
    cache_k2 = cache_k.reshape(DEPTH, bsz, past * N_HEADS, HEAD_W)
    cache_v2 = cache_v.reshape(DEPTH, bsz, past * N_HEADS, HEAD_W)
    rows = 2 * t * N_HEADS
    block_bytes = 2 * t * width * 2 + 2 * t * width * 4 + 2 * past * width * 4
    temp_bytes = rows * past * N_HEADS * (4 + 2) + 2 * past * width * 2
    return pl.pallas_call(
        functools.partial(_sample_attn_kernel, t=t),
        grid_spec=pltpu.PrefetchScalarGridSpec(
            num_scalar_prefetch=1,
            grid=(bsz,),
            in_specs=[pl.BlockSpec((5, HEAD_DIM), lambda b, layer_ref: (0, 0)),
                      pl.BlockSpec((1, HEAD_W), lambda b, layer_ref: (0, 0)),
                      q_spec, new_spec, new_spec, cache_spec, cache_spec],
            out_specs=q_spec,
            scratch_shapes=[pltpu.VMEM((rows, HEAD_W), BF16)]),
        out_shape=jax.ShapeDtypeStruct((bsz, t, width), BF16),
        compiler_params=pltpu.CompilerParams(
            dimension_semantics=("arbitrary",),
            vmem_limit_bytes=_vmem_limit(block_bytes, temp_bytes)),
        name="sample_attention",
    )(jnp.full((1,), layer, jnp.int32), lam5, subw, q, k_new, v_new, cache_k2, cache_v2)


def _out_kernel(px_ref, hist_ref, pg_ref, o_ref, ag_ref, ga_ref, gb_ref, x_ref,
                wp_ref, ps_ref, wa_ref, wb_ref, wo_ref, g_ref, b_ref,
                y_ref, ext_ref, lvl_ref, *, nb, tm, pos0, first_tile_has_no_history):
    i = pl.program_id(1)
    rows = nb * tm
    hist = hist_ref[...]
    if first_tile_has_no_history:
        hist = jnp.where(i == 0, 0.0, hist)
    ext_ref[:, 0:HIST_ROWS, :] = hist
    ext_ref[:, HIST_ROWS:HIST_ROWS + tm, :] = px_ref[...]

    lo = HIST_ROWS // 2
    n = HIST_ROWS - lo + tm
    max_shift = max(POOL_WINDOWS) // 2
    lvl_ref[:, :, lo - max_shift:lo, :] = jnp.zeros((2, nb, max_shift, POOL_GROUP), F32)

    pos = pos0 + i * tm + lax.broadcasted_iota(jnp.int32, (nb, tm, POOL_GROUP), 1)
    ya_parts = []
    for g, w in enumerate(POOL_WINDOWS):
        sl = slice(g * POOL_GROUP, (g + 1) * POOL_GROUP)
        u = ext_ref[:, HIST_ROWS:HIST_ROWS + tm, sl]
        levels = w.bit_length() - 1
        for k in range(levels):
            sh = 1 << k
            if k == 0:
                win = ext_ref[:, lo:lo + n, sl] + ext_ref[:, lo - sh:lo - sh + n, sl]
            else:
                src = lvl_ref.at[(k - 1) % 2]
                win = src[:, lo:lo + n, :] + src[:, lo - sh:lo - sh + n, :]
            if k < levels - 1:
                lvl_ref[k % 2, :, lo:lo + n, :] = win
        s = win[:, HIST_ROWS - lo:, :]
        cnt = jnp.minimum(pos + 1, w).astype(F32)
        pooled = (s / cnt - u).reshape(rows, POOL_GROUP)
        ya_parts.append(jnp.dot(pooled.astype(BF16), wp_ref[g], preferred_element_type=F32))
    ya = (jnp.concatenate(ya_parts, axis=1) * ps_ref[...]).astype(BF16)
    ya = ya * _silu(pg_ref[...].reshape(rows, D_POOL))
    yb = o_ref[...].reshape(rows, D_MODEL) * _silu(ag_ref[...].reshape(rows, D_MODEL))

    ma = jnp.dot(ya, wa_ref[...], preferred_element_type=F32)
    mb = jnp.dot(yb, wb_ref[...], preferred_element_type=F32)
    merged = (_sigmoid(ga_ref[...].reshape(rows, D_MODEL)) * ma.astype(BF16)
              + _sigmoid(gb_ref[...].reshape(rows, D_MODEL)) * mb.astype(BF16))
    out = jnp.dot(merged, wo_ref[...], preferred_element_type=F32)
    z = ALPHA * x_ref[...].reshape(rows, D_MODEL) + out
    y_ref[...] = _layer_norm_rows(z, g_ref[...], b_ref[...]).reshape(nb, tm, D_MODEL)


def _out_projection(px, hist, pg, o, ag, ga, gb, x, wp, ps, wa, wb, wo, g, b, *,
                    nb, tm, pos0, first_tile_has_no_history):
    bsz, seq, _ = px.shape
    hist_blocks_per_tile = tm // HIST_ROWS

    def act_spec(width):
        return pl.BlockSpec((nb, tm, width), lambda bi, i: (bi, i, 0))

    def whole(shape):
        return pl.BlockSpec(shape, lambda bi, i: (0,) * len(shape), pipeline_mode=pl.Buffered(1))

    hist_spec = pl.BlockSpec(
        (nb, HIST_ROWS, D_POOL),
        lambda bi, i: (bi, jnp.maximum(i * hist_blocks_per_tile - 1, 0), 0))
    rows = nb * tm
    block_bytes = rows * (D_POOL * 4 + D_POOL * 2 + 4 * D_MODEL * 2 + 2 * D_MODEL * 4)
    weight_bytes = (D_POOL + 2 * D_MODEL) * D_MODEL * 2 + rows * D_POOL * 4
    return pl.pallas_call(
        functools.partial(_out_kernel, nb=nb, tm=tm, pos0=pos0,
                          first_tile_has_no_history=first_tile_has_no_history),
        grid=(bsz // nb, seq // tm),
        in_specs=[act_spec(D_POOL), hist_spec, act_spec(D_POOL), act_spec(D_MODEL),
                  act_spec(D_MODEL), act_spec(D_MODEL), act_spec(D_MODEL), act_spec(D_MODEL),
                  whole((len(POOL_WINDOWS), POOL_GROUP, POOL_GROUP)), whole((1, D_POOL)),
                  whole((D_POOL, D_MODEL)), whole((D_MODEL, D_MODEL)), whole((D_MODEL, D_MODEL)),
                  whole((1, D_MODEL)), whole((1, D_MODEL))],
        out_specs=act_spec(D_MODEL),
        out_shape=jax.ShapeDtypeStruct((bsz, seq, D_MODEL), F32),
        scratch_shapes=[pltpu.VMEM((nb, HIST_ROWS + tm, D_POOL), F32),
                        pltpu.VMEM((2, nb, HIST_ROWS + tm, POOL_GROUP), F32)],
        compiler_params=pltpu.CompilerParams(
            dimension_semantics=("arbitrary", "arbitrary"),
            vmem_limit_bytes=_vmem_limit(block_bytes, weight_bytes)),
        name="out_projection",
    )(px, hist, pg, o, ag, ga, gb, x, wp, ps, wa, wb, wo, g, b)


def kernel(x_prompt, x_sample, cache_k, cache_v, state_pool, ln_in_g, ln_in_b, w_in, w_pool,
           pool_scale, lambda_qk, subln_w, w_a, w_b, w_o, ln_g, ln_b):
    bsz, seq, _ = x_prompt.shape
    dbsz, dseq, _ = x_sample.shape
    past = cache_k.shape[2]

    cos_p, sin_p = _rope_tables(seq, 0, seq)
    cos_s, sin_s = _rope_tables(dbsz * dseq, past, dseq)

    xp = _entry_layer_norm(x_prompt.reshape(bsz * seq, D_MODEL), ln_in_g, ln_in_b)
    xs = _entry_layer_norm(x_sample.reshape(dbsz * dseq, D_MODEL), ln_in_g, ln_in_b)

    state_rows = jnp.pad(state_pool, ((0, 0), (0, 0), (HIST_ROWS - POOL_HIST, 0), (0, 0)))

    kp_all = jnp.zeros((DEPTH, bsz * seq * N_HEADS, HEAD_W), F32)
    vp_all = jnp.zeros((DEPTH, bsz * seq * N_HEADS, HEAD_W), F32)
    ks_all = jnp.zeros((DEPTH, dbsz * dseq * N_HEADS, HEAD_W), F32)
    vs_all = jnp.zeros((DEPTH, dbsz * dseq * N_HEADS, HEAD_W), F32)

    hp_l, hs_l = [], []
    for l in range(DEPTH):
        lam_init = 0.8 - 0.6 * math.exp(-0.3 * l)
        lam5 = jnp.concatenate([lambda_qk[l].astype(F32), jnp.full((1, HEAD_DIM), lam_init, F32)], axis=0)
        w_in_b = w_in[l].astype(BF16)
        wp_b, wa_b, wb_b, wo_b = (w_pool[l].astype(BF16), w_a[l].astype(BF16),
                                  w_b[l].astype(BF16), w_o[l].astype(BF16))
        ps = pool_scale[l].reshape(1, D_POOL)
        subw = subln_w[l].reshape(1, HEAD_W)
        g = ln_g[l].reshape(1, D_MODEL)
        b = ln_b[l].reshape(1, D_MODEL)

        px, pg, q, kb, vb, ag, ga, gb, kp_all, vp_all = _in_projection(
            l, xp, w_in_b, cos_p, sin_p, kp_all, vp_all)
        r3 = lambda a: a.reshape(bsz, seq, a.shape[-1])
        o = _prompt_attention(r3(q), r3(kb), r3(vb), lam5, subw)
        px3 = r3(px)
        xp = _out_projection(px3, px3, r3(pg), o, r3(ag), r3(ga), r3(gb), r3(xp),
                             wp_b, ps, wa_b, wb_b, wo_b, g, b,
                             nb=1, tm=ROW_TILE, pos0=0,
                             first_tile_has_no_history=True).reshape(bsz * seq, D_MODEL)
        hp_l.append(px3[:, seq - POOL_HIST:, :])

        px, pg, q, kb, vb, ag, ga, gb, ks_all, vs_all = _in_projection(
            l, xs, w_in_b, cos_s, sin_s, ks_all, vs_all)
        s3 = lambda a: a.reshape(dbsz, dseq, a.shape[-1])
        o = _sample_attention(l, s3(q), ks_all, vs_all, cache_k, cache_v, lam5, subw)
        px3 = s3(px)
        xs = _out_projection(px3, state_rows[l], s3(pg), o, s3(ag), s3(ga), s3(gb), s3(xs),
                             wp_b, ps, wa_b, wb_b, wo_b, g, b,
                             nb=dbsz, tm=dseq, pos0=past,
                             first_tile_has_no_history=False).reshape(dbsz * dseq, D_MODEL)
        hs_l.append(jnp.concatenate([state_pool[l], px3], axis=1)[:, -POOL_HIST:, :])

    return (xp.reshape(bsz, seq, D_MODEL), xs.reshape(dbsz, dseq, D_MODEL),
            kp_all.reshape(DEPTH, bsz, seq, N_HEADS, HEAD_W),
            vp_all.reshape(DEPTH, bsz, seq, N_HEADS, HEAD_W),
            jnp.stack(hp_l),
            ks_all.reshape(DEPTH, dbsz, dseq, N_HEADS, HEAD_W),
            vs_all.reshape(DEPTH, dbsz, dseq, N_HEADS, HEAD_W),
            jnp.stack(hs_l))
```

```python
import functools
import math

import jax
import jax.numpy as jnp
from jax import lax
from jax.experimental import pallas as pl
from jax.experimental.pallas import tpu as pltpu

F32 = jnp.float32
BF16 = jnp.bfloat16

D_MODEL = 1024
DEPTH = 4
CHUNK = 64
CHUNK_SHIFT = 6
D_POOL = 512
POOL_WINDOWS = (2, 4, 8, 16)
POOL_GROUP = 128
POOL_HIST = 15
HIST_ROWS = 32
N_HEADS = 8
HEAD_DIM = 64
HEAD_W = 2 * HEAD_DIM
ROPE_HALF = HEAD_DIM // 2
ROPE_THETA = 10000.0
QK_SCALE = HEAD_DIM ** -0.5
Q_PRESCALE = QK_SCALE * math.log2(math.e)
LN_EPS = 1e-5
RMS_EPS = 1e-5
ALPHA = (2 * DEPTH) ** 0.25
COL_PX, COL_PG, COL_Q, COL_K, COL_V, COL_AG, COL_GA, COL_GB = 0, 512, 1024, 2048, 3072, 4096, 5120, 6144
D_IN = 7168

V7X_VMEM_BYTES = 64 * 1024 * 1024
LANES = 128

MASKED = -1e30
ROW_TILE = 512
ATTN_TILE = 512
ATTN_WIDE_KEYS = 1024
ATTN_HEADS_PER_STEP = 4
TEMP_VMEM_BYTES = 16 << 20


def _vmem_limit(pipelined_bytes, resident_bytes=0):
    return min(2 * pipelined_bytes + resident_bytes + TEMP_VMEM_BYTES, V7X_VMEM_BYTES - (4 << 20))


def _sigmoid(x):
    return 1.0 / (1.0 + jnp.exp(-x))


def _silu(x):
    return x * _sigmoid(x)


def _layer_norm_rows(z, g, b):
    mu = jnp.mean(z, axis=-1, keepdims=True)
    zc = z - mu
    var = jnp.mean(zc * zc, axis=-1, keepdims=True)
    return zc * lax.rsqrt(var + LN_EPS) * g + b


def _ln_kernel(x_ref, g_ref, b_ref, y_ref):
    y_ref[...] = _layer_norm_rows(x_ref[...], g_ref[...], b_ref[...])


def _entry_layer_norm(x2d, g, b):
    rows = x2d.shape[0]
    tm = min(ROW_TILE, rows)
    row_spec = pl.BlockSpec((tm, D_MODEL), lambda i: (i, 0))
    vec_spec = pl.BlockSpec((1, D_MODEL), lambda i: (0, 0))
    return pl.pallas_call(
        _ln_kernel,
        grid=(rows // tm,),
        in_specs=[row_spec, vec_spec, vec_spec],
        out_specs=row_spec,
        out_shape=jax.ShapeDtypeStruct((rows, D_MODEL), F32),
        compiler_params=pltpu.CompilerParams(
            dimension_semantics=("arbitrary",),
            vmem_limit_bytes=_vmem_limit(2 * tm * D_MODEL * 4)),
        name="entry_ln",
    )(x2d, g.reshape(1, D_MODEL), b.reshape(1, D_MODEL))


def _rope_table_kernel(cos_ref, sin_ref, *, pos0, period, tm):
    i = pl.program_id(0)
    lane = lax.broadcasted_iota(jnp.int32, (tm, LANES), 1)
    row = lax.broadcasted_iota(jnp.int32, (tm, LANES), 0) + i * tm
    freq = (lane & (ROPE_HALF - 1)).astype(F32)
    inv = jnp.exp(freq * (-math.log(ROPE_THETA) / ROPE_HALF))
    pos = (pos0 + (row & (period - 1))).astype(F32)
    ang = pos * inv
    sign = jnp.where((lane & ROPE_HALF) == 0, -1.0, 1.0)
    cos_ref[...] = jnp.cos(ang)
    sin_ref[...] = jnp.sin(ang) * sign


def _rope_tables(rows, pos0, period):
    tm = min(ROW_TILE, rows)
    spec = pl.BlockSpec((tm, LANES), lambda i: (i, 0))
    shape = jax.ShapeDtypeStruct((rows, LANES), F32)
    return pl.pallas_call(
        functools.partial(_rope_table_kernel, pos0=pos0, period=period, tm=tm),
        grid=(rows // tm,),
        out_specs=[spec, spec],
        out_shape=[shape, shape],
        compiler_params=pltpu.CompilerParams(dimension_semantics=("arbitrary",)),
        name="rope_tables",
    )()


def _inproj_kernel(layer_ref, x_ref, w_ref, cos_ref, sin_ref, k_all_in, v_all_in,
                   px_ref, pg_ref, q_ref, kb_ref, vb_ref, ag_ref, ga_ref, gb_ref, k_ref, v_ref,
                   xb_ref):
    del layer_ref, k_all_in, v_all_in
    tm = x_ref.shape[0]
    xb_ref[...] = x_ref[...].astype(BF16)

    def proj(col, width):
        return jnp.dot(xb_ref[...], w_ref[:, col:col + width], preferred_element_type=F32)

    def head_rows(h):
        return pl.ds(h, tm, stride=N_HEADS)

    px_ref[...] = proj(COL_PX, D_POOL)
    pg_ref[...] = proj(COL_PG, D_POOL).astype(BF16)

    cos = cos_ref[...]
    sin = sin_ref[...]
    lane = lax.broadcasted_iota(jnp.int32, (tm, LANES), 1)
    holds_x1 = (lane & ROPE_HALF) == 0

    def rope(y):
        partner = jnp.where(holds_x1,
                            pltpu.roll(y, LANES - ROPE_HALF, 1),
                            pltpu.roll(y, ROPE_HALF, 1))
        return y * cos + partner * sin

    q = proj(COL_Q, D_MODEL)
    for h in range(N_HEADS):
        sl = slice(h * HEAD_W, (h + 1) * HEAD_W)
        q_ref[:, sl] = (rope(q[:, sl]) * Q_PRESCALE).astype(BF16)
    k = proj(COL_K, D_MODEL)
    for h in range(N_HEADS):
        sl = slice(h * HEAD_W, (h + 1) * HEAD_W)
        kr = rope(k[:, sl])
        k_ref[0, head_rows(h), :] = kr
        kb_ref[:, sl] = kr.astype(BF16)
    v = proj(COL_V, D_MODEL)
    for h in range(N_HEADS):
        sl = slice(h * HEAD_W, (h + 1) * HEAD_W)
        v_ref[0, head_rows(h), :] = v[:, sl]
    vb_ref[...] = v.astype(BF16)
    ag_ref[...] = proj(COL_AG, D_MODEL).astype(BF16)
    ga_ref[...] = proj(COL_GA, D_MODEL).astype(BF16)
    gb_ref[...] = proj(COL_GB, D_MODEL).astype(BF16)


def _in_projection(layer, x2d, w_bf16, cos_t, sin_t, k_all, v_all):
    rows = x2d.shape[0]
    slab_shape = jax.ShapeDtypeStruct((DEPTH, rows * N_HEADS, HEAD_W), F32)
    tm = min(ROW_TILE, rows)
    table_blocks = cos_t.shape[0] // tm

    def rows_spec(width):
        return pl.BlockSpec((tm, width), lambda i, layer_ref: (i, 0))

    table_spec = pl.BlockSpec((tm, LANES), lambda i, layer_ref: (i % table_blocks, 0))
    w_spec = pl.BlockSpec((None, D_MODEL, D_IN), lambda i, layer_ref: (layer_ref[0], 0, 0),
                          pipeline_mode=pl.Buffered(1))
    kv_out_spec = pl.BlockSpec((1, tm * N_HEADS, HEAD_W), lambda i, layer_ref: (layer_ref[0], i, 0))
    any_spec = pl.BlockSpec(memory_space=pl.ANY)
    outs = [(D_POOL, F32), (D_POOL, BF16), (D_MODEL, BF16), (D_MODEL, BF16), (D_MODEL, BF16),
            (D_MODEL, BF16), (D_MODEL, BF16), (D_MODEL, BF16)]
    out_row_bytes = sum(wd * jnp.dtype(dt).itemsize for wd, dt in outs) + 2 * D_MODEL * 4
    block_bytes = tm * (D_MODEL * 4 + 2 * LANES * 4 + out_row_bytes)
    res = pl.pallas_call(
        _inproj_kernel,
        grid_spec=pltpu.PrefetchScalarGridSpec(
            num_scalar_prefetch=1,
            grid=(rows // tm,),
            in_specs=[rows_spec(D_MODEL), w_spec, table_spec, table_spec, any_spec, any_spec],
            out_specs=[rows_spec(wd) for wd, _ in outs] + [kv_out_spec, kv_out_spec],
            scratch_shapes=[pltpu.VMEM((tm, D_MODEL), BF16)]),
        out_shape=[jax.ShapeDtypeStruct((rows, wd), dt) for wd, dt in outs] + [slab_shape, slab_shape],
        input_output_aliases={5: 8, 6: 9},
        compiler_params=pltpu.CompilerParams(
            dimension_semantics=("arbitrary",),
            vmem_limit_bytes=_vmem_limit(block_bytes, D_MODEL * D_IN * 2 + tm * D_MODEL * 2)),
        name="in_projection",
    )(jnp.full((1,), layer, jnp.int32), x2d, w_bf16, cos_t, sin_t, k_all, v_all)
    return res


def _stack_maps(q, q2_ref):
    t = q.shape[0]
    lane = lax.broadcasted_iota(jnp.int32, q.shape, 1)
    zero = jnp.zeros_like(q)
    q2_ref[0:t, :] = jnp.where(lane < HEAD_DIM, q, zero)
    q2_ref[t:2 * t, :] = jnp.where(lane >= HEAD_DIM, q, zero)


def _scores(q2, k_blk):
    return lax.dot_general(q2, k_blk, (((1,), (1,)), ((), ())), preferred_element_type=F32)


def _lambda(lam_ref):
    lq = lam_ref[...]
    d1 = jnp.sum(lq[0:1, :] * lq[1:2, :], axis=1, keepdims=True)
    d2 = jnp.sum(lq[2:3, :] * lq[3:4, :], axis=1, keepdims=True)
    lam_init = lq[4:5, 0:1]
    return jnp.exp(d1) - jnp.exp(d2) + lam_init, lam_init


def _finish_head(acc, l, t, lam, lam_init, subw):
    o1 = acc[0:t, :] / l[0:t, :]
    o2 = acc[t:2 * t, :] / l[t:2 * t, :]
    o = o1 - lam * o2
    ms = jnp.mean(o * o, axis=1, keepdims=True)
    return o * lax.rsqrt(ms + RMS_EPS) * subw * (1.0 - lam_init)


def _prompt_attn_kernel(lam_ref, subw_ref, q_ref, k_ref, v_ref, o_ref,
                        q2_ref, vx_ref, m_ref, acc_ref, *, t, heads):
    qi = pl.program_id(2)
    hf = t // 2

    @pl.when(qi == 0)
    def _():
        for g in range(heads):
            vx_ref[g, :, 0:HEAD_W] = v_ref[0, :, g * HEAD_W:(g + 1) * HEAD_W]
            vx_ref[g, :, HEAD_W:2 * HEAD_W] = jnp.ones((vx_ref.shape[1], HEAD_W), BF16)

    for g in range(heads):
        lanes = slice(g * HEAD_W, (g + 1) * HEAD_W)
        _stack_maps(q_ref[0, 0:hf, lanes], q2_ref.at[g, 0:t])
        _stack_maps(q_ref[0, hf:t, lanes], q2_ref.at[g, t:2 * t])
    m_ref[...] = jnp.full(m_ref.shape, MASKED, F32)
    acc_ref[...] = jnp.zeros(acc_ref.shape, F32)

    half_row_chunk = (lax.broadcasted_iota(jnp.int32, (t, LANES), 0) & (hf - 1)) >> CHUNK_SHIFT
    lane_chunk = lax.broadcasted_iota(jnp.int32, (t, LANES), 1) >> CHUNK_SHIFT
    chunks_per_slab = LANES // CHUNK
    own_bias = [jnp.where(lane_chunk + cb * chunks_per_slab > half_row_chunk, MASKED, 0.0)
                for cb in range(hf // LANES)]

    def head_step(g, off, width, row0=0, nrows=2 * t, masked_from=None):
        lanes = slice(g * HEAD_W, (g + 1) * HEAD_W)
        rows = slice(row0, row0 + nrows)
        s = _scores(q2_ref[g, rows], k_ref[0, pl.ds(off, width), lanes])
        slabs = [s[:, cb * LANES:(cb + 1) * LANES] for cb in range(width // LANES)]
        if masked_from is not None:
            slabs = slabs[:masked_from] + [
                slabs[cb] + own_bias[cb - masked_from] for cb in range(masked_from, len(slabs))]
        top = slabs[0]
        for sb in slabs[1:]:
            top = jnp.maximum(top, sb)
        m_prev = m_ref[g, rows]
        m_new = jnp.maximum(m_prev, jnp.max(top, axis=1, keepdims=True))
        alpha = jnp.exp2(m_prev - m_new)
        p = jnp.concatenate([jnp.exp2(sb - m_new) for sb in slabs], axis=1).astype(BF16)
        m_ref[g, rows] = m_new
        pv = jnp.dot(p, vx_ref[g, pl.ds(off, width), :], preferred_element_type=F32)
        for part in range(2):
            cols = slice(part * HEAD_W, (part + 1) * HEAD_W)
            acc_ref[g, rows, cols] = alpha * acc_ref[g, rows, cols] + pv[:, cols]

    def block_step(off, width):
        for g in range(heads):
            head_step(g, off, width)

    ww = ATTN_WIDE_KEYS
    assert ww in (t, 2 * t)
    n_wide = lax.shift_right_logical(qi * t, ww.bit_length() - 1)

    def wide_pair(f, carry):
        block_step(pl.multiple_of(f * 2 * ww, 2 * ww), ww)
        block_step(pl.multiple_of(f * 2 * ww + ww, ww), ww)
        return carry

    lax.fori_loop(0, lax.shift_right_logical(n_wide, 1), wide_pair, 0)

    @pl.when((n_wide & 1) == 1)
    def _():
        block_step(pl.multiple_of((n_wide - 1) * ww, ww), ww)

    if t < ww:
        @pl.when((qi & 1) == 1)
        def _():
            block_step(pl.multiple_of((qi - 1) * t, t), t)

    own = pl.multiple_of(qi * t, t)
    slabs_per_half = hf // LANES
    for g in range(heads):
        head_step(g, own, hf, row0=0, nrows=t, masked_from=0)
        head_step(g, own, t, row0=t, nrows=t, masked_from=slabs_per_half)

    lam, lam_init = _lambda(lam_ref)
    for g in range(heads):
        for half in range(2):
            rows = slice(half * t, (half + 1) * t)
            o_ref[0, half * hf:(half + 1) * hf, g * HEAD_W:(g + 1) * HEAD_W] = _finish_head(
                acc_ref[g, rows, 0:HEAD_W], acc_ref[g, rows, HEAD_W:2 * HEAD_W],
                hf, lam, lam_init, subw_ref[...]).astype(BF16)


def _prompt_attention(q, kb, vb, lam5, subw):
    bsz, seq, width = q.shape
    t = ATTN_TILE
    heads = ATTN_HEADS_PER_STEP
    q_spec = pl.BlockSpec((1, t, heads * HEAD_W), lambda b, h, i: (b, i, h))
    kv_spec = pl.BlockSpec((1, seq, heads * HEAD_W), lambda b, h, i: (b, 0, h))
    block_bytes = heads * (2 * t * HEAD_W * 2 + 2 * seq * HEAD_W * 2)
    scratch_bytes = heads * (2 * t * HEAD_W * 2 + seq * 2 * HEAD_W * 2 + 2 * t * (LANES + 2 * HEAD_W) * 4)
    return pl.pallas_call(
        functools.partial(_prompt_attn_kernel, t=t, heads=heads),
        grid=(bsz, N_HEADS // heads, seq // t),
        in_specs=[pl.BlockSpec((5, HEAD_DIM), lambda b, h, i: (0, 0)),
                  pl.BlockSpec((1, HEAD_W), lambda b, h, i: (0, 0)),
                  q_spec, kv_spec, kv_spec],
        out_specs=q_spec,
        out_shape=jax.ShapeDtypeStruct((bsz, seq, width), BF16),
        scratch_shapes=[pltpu.VMEM((heads, 2 * t, HEAD_W), BF16),
                        pltpu.VMEM((heads, seq, 2 * HEAD_W), BF16),
                        pltpu.VMEM((heads, 2 * t, LANES), F32),
                        pltpu.VMEM((heads, 2 * t, 2 * HEAD_W), F32)],
        compiler_params=pltpu.CompilerParams(
            dimension_semantics=("arbitrary", "arbitrary", "arbitrary"),
            vmem_limit_bytes=_vmem_limit(block_bytes, scratch_bytes)),
        name="prompt_attention",
    )(lam5, subw, q, kb, vb)


def _sample_attn_kernel(layer_ref, lam_ref, subw_ref, q_ref, kn_ref, vn_ref, kc_ref, vc_ref, o_ref,
                        q2_ref, *, t):
    del layer_ref
    rows_per_head = 2 * t
    rows = N_HEADS * rows_per_head
    for h in range(N_HEADS):
        _stack_maps(q_ref[0, :, h * HEAD_W:(h + 1) * HEAD_W],
                    q2_ref.at[h * rows_per_head:(h + 1) * rows_per_head])
    q2 = q2_ref[...]
    s_c = _scores(q2, kc_ref[0, 0].astype(BF16))
    s_n = _scores(q2, kn_ref[0].astype(BF16))

    r = lax.broadcasted_iota(jnp.int32, (rows, LANES), 0)
    c = lax.broadcasted_iota(jnp.int32, (rows, LANES), 1)
    head_shift = rows_per_head.bit_length() - 1
    other_head = (c & (N_HEADS - 1)) != (r >> head_shift)

    def slabs_of(s):
        return [jnp.where(other_head, MASKED, s[:, cb * LANES:(cb + 1) * LANES])
                for cb in range(s.shape[1] // LANES)]

    slabs_c = slabs_of(s_c)
    slabs_n = slabs_of(s_n)
    top = slabs_n[0]
    for sb in slabs_n[1:] + slabs_c:
        top = jnp.maximum(top, sb)
    m = jnp.max(top, axis=1, keepdims=True)
    ps_c = [jnp.exp2(sb - m) for sb in slabs_c]
    ps_n = [jnp.exp2(sb - m) for sb in slabs_n]
    psum = ps_n[0]
    for pc in ps_n[1:] + ps_c:
        psum = psum + pc
    l_all = jnp.sum(psum, axis=1, keepdims=True)
    acc_all = (jnp.dot(jnp.concatenate(ps_c, axis=1).astype(BF16), vc_ref[0, 0].astype(BF16),
                       preferred_element_type=F32)
               + jnp.dot(jnp.concatenate(ps_n, axis=1).astype(BF16), vn_ref[0].astype(BF16),
                         preferred_element_type=F32))

    lam, lam_init = _lambda(lam_ref)
    subw = subw_ref[...]
    for h in range(N_HEADS):
        hr = slice(h * rows_per_head, (h + 1) * rows_per_head)
        o_ref[0, :, h * HEAD_W:(h + 1) * HEAD_W] = _finish_head(
            acc_all[hr, :], l_all[hr, :], t, lam, lam_init, subw).astype(BF16)


def _sample_attention(layer, q, k_new, v_new, cache_k, cache_v, lam5, subw):
    bsz, t, width = q.shape
    past = cache_k.shape[2]
    q_spec = pl.BlockSpec((1, t, width), lambda b, layer_ref: (b, 0, 0))
    new_spec = pl.BlockSpec((1, t * N_HEADS, HEAD_W), lambda b, layer_ref: (layer_ref[0], b, 0))
    cache_spec = pl.BlockSpec((1, 1, past * N_HEADS, HEAD_W),
                              lambda b, layer_ref: (layer_ref[0], b, 0, 0))
    cache_k2 = cache_k.reshape(DEPTH, bsz, past * N_HEADS, HEAD_W)
    cache_v2 = cache_v.reshape(DEPTH, bsz, past * N_HEADS, HEAD_W)
    rows = 2 * t * N_HEADS
    block_bytes = 2 * t * width * 2 + 2 * t * width * 4 + 2 * past * width * 4
    temp_bytes = rows * past * N_HEADS * (4 + 2) + 2 * past * width * 2
    return pl.pallas_call(
        functools.partial(_sample_attn_kernel, t=t),
        grid_spec=pltpu.PrefetchScalarGridSpec(
            num_scalar_prefetch=1,
            grid=(bsz,),
            in_specs=[pl.BlockSpec((5, HEAD_DIM), lambda b, layer_ref: (0, 0)),
                      pl.BlockSpec((1, HEAD_W), lambda b, layer_ref: (0, 0)),
                      q_spec, new_spec, new_spec, cache_spec, cache_spec],
            out_specs=q_spec,
            scratch_shapes=[pltpu.VMEM((rows, HEAD_W), BF16)]),
        out_shape=jax.ShapeDtypeStruct((bsz, t, width), BF16),
        compiler_params=pltpu.CompilerParams(
            dimension_semantics=("arbitrary",),
            vmem_limit_bytes=_vmem_limit(block_bytes, temp_bytes)),
        name="sample_attention",
    )(jnp.full((1,), layer, jnp.int32), lam5, subw, q, k_new, v_new, cache_k2, cache_v2)


def _out_kernel(px_ref, hist_ref, pg_ref, o_ref, ag_ref, ga_ref, gb_ref, x_ref,
                wp_ref, ps_ref, wa_ref, wb_ref, wo_ref, g_ref, b_ref,
                y_ref, ext_ref, lvl_ref, *, nb, tm, pos0, first_tile_has_no_history):
    i = pl.program_id(1)
    rows = nb * tm
    hist = hist_ref[...]
    if first_tile_has_no_history:
        hist = jnp.where(i == 0, 0.0, hist)
    ext_ref[:, 0:HIST_ROWS, :] = hist
    ext_ref[:, HIST_ROWS:HIST_ROWS + tm, :] = px_ref[...]

    lo = HIST_ROWS // 2
    n = HIST_ROWS - lo + tm
    max_shift = max(POOL_WINDOWS) // 2
    lvl_ref[:, :, lo - max_shift:lo, :] = jnp.zeros((2, nb, max_shift, POOL_GROUP), F32)

    pos = pos0 + i * tm + lax.broadcasted_iota(jnp.int32, (nb, tm, POOL_GROUP), 1)
    ya_parts = []
    for g, w in enumerate(POOL_WINDOWS):
        sl = slice(g * POOL_GROUP, (g + 1) * POOL_GROUP)
        u = ext_ref[:, HIST_ROWS:HIST_ROWS + tm, sl]
        levels = w.bit_length() - 1
        for k in range(levels):
            sh = 1 << k
            if k == 0:
                win = ext_ref[:, lo:lo + n, sl] + ext_ref[:, lo - sh:lo - sh + n, sl]
            else:
                src = lvl_ref.at[(k - 1) % 2]
                win = src[:, lo:lo + n, :] + src[:, lo - sh:lo - sh + n, :]
            if k < levels - 1:
                lvl_ref[k % 2, :, lo:lo + n, :] = win
        s = win[:, HIST_ROWS - lo:, :]
        cnt = jnp.minimum(pos + 1, w).astype(F32)
        pooled = (s / cnt - u).reshape(rows, POOL_GROUP)
        ya_parts.append(jnp.dot(pooled.astype(BF16), wp_ref[g], preferred_element_type=F32))
    ya = (jnp.concatenate(ya_parts, axis=1) * ps_ref[...]).astype(BF16)
    ya = ya * _silu(pg_ref[...].reshape(rows, D_POOL))
    yb = o_ref[...].reshape(rows, D_MODEL) * _silu(ag_ref[...].reshape(rows, D_MODEL))

    ma = jnp.dot(ya, wa_ref[...], preferred_element_type=F32)
    mb = jnp.dot(yb, wb_ref[...], preferred_element_type=F32)
    merged = (_sigmoid(ga_ref[...].reshape(rows, D_MODEL)) * ma.astype(BF16)
              + _sigmoid(gb_ref[...].reshape(rows, D_MODEL)) * mb.astype(BF16))
    out = jnp.dot(merged, wo_ref[...], preferred_element_type=F32)
    z = ALPHA * x_ref[...].reshape(rows, D_MODEL) + out
    y_ref[...] = _layer_norm_rows(z, g_ref[...], b_ref[...]).reshape(nb, tm, D_MODEL)


def _out_projection(px, hist, pg, o, ag, ga, gb, x, wp, ps, wa, wb, wo, g, b, *,
                    nb, tm, pos0, first_tile_has_no_history):
    bsz, seq, _ = px.shape
    hist_blocks_per_tile = tm // HIST_ROWS

    def act_spec(width):
        return pl.BlockSpec((nb, tm, width), lambda bi, i: (bi, i, 0))

    def whole(shape):
        return pl.BlockSpec(shape, lambda bi, i: (0,) * len(shape), pipeline_mode=pl.Buffered(1))

    hist_spec = pl.BlockSpec(
        (nb, HIST_ROWS, D_POOL),
        lambda bi, i: (bi, jnp.maximum(i * hist_blocks_per_tile - 1, 0), 0))
    rows = nb * tm
    block_bytes = rows * (D_POOL * 4 + D_POOL * 2 + 4 * D_MODEL * 2 + 2 * D_MODEL * 4)
    weight_bytes = (D_POOL + 2 * D_MODEL) * D_MODEL * 2 + rows * D_POOL * 4
    return pl.pallas_call(
        functools.partial(_out_kernel, nb=nb, tm=tm, pos0=pos0,
                          first_tile_has_no_history=first_tile_has_no_history),
        grid=(bsz // nb, seq // tm),
        in_specs=[act_spec(D_POOL), hist_spec, act_spec(D_POOL), act_spec(D_MODEL),
                  act_spec(D_MODEL), act_spec(D_MODEL), act_spec(D_MODEL), act_spec(D_MODEL),
                  whole((len(POOL_WINDOWS), POOL_GROUP, POOL_GROUP)), whole((1, D_POOL)),
                  whole((D_POOL, D_MODEL)), whole((D_MODEL, D_MODEL)), whole((D_MODEL, D_MODEL)),
                  whole((1, D_MODEL)), whole((1, D_MODEL))],
        out_specs=act_spec(D_MODEL),
        out_shape=jax.ShapeDtypeStruct((bsz, seq, D_MODEL), F32),
        scratch_shapes=[pltpu.VMEM((nb, HIST_ROWS + tm, D_POOL), F32),
                        pltpu.VMEM((2, nb, HIST_ROWS + tm, POOL_GROUP), F32)],
        compiler_params=pltpu.CompilerParams(
            dimension_semantics=("arbitrary", "arbitrary"),
            vmem_limit_bytes=_vmem_limit(block_bytes, weight_bytes)),
        name="out_projection",
    )(px, hist, pg, o, ag, ga, gb, x, wp, ps, wa, wb, wo, g, b)


def kernel(x_prompt, x_sample, cache_k, cache_v, state_pool, ln_in_g, ln_in_b, w_in, w_pool,
           pool_scale, lambda_qk, subln_w, w_a, w_b, w_o, ln_g, ln_b):
    bsz, seq, _ = x_prompt.shape
    dbsz, dseq, _ = x_sample.shape
    past = cache_k.shape[2]

    cos_p, sin_p = _rope_tables(seq, 0, seq)
    cos_s, sin_s = _rope_tables(dbsz * dseq, past, dseq)

    xp = _entry_layer_norm(x_prompt.reshape(bsz * seq, D_MODEL), ln_in_g, ln_in_b)
    xs = _entry_layer_norm(x_sample.reshape(dbsz * dseq, D_MODEL), ln_in_g, ln_in_b)

    state_rows = jnp.pad(state_pool, ((0, 0), (0, 0), (HIST_ROWS - POOL_HIST, 0), (0, 0)))

    kp_all = jnp.zeros((DEPTH, bsz * seq * N_HEADS, HEAD_W), F32)
    vp_all = jnp.zeros((DEPTH, bsz * seq * N_HEADS, HEAD_W), F32)
    ks_all = jnp.zeros((DEPTH, dbsz * dseq * N_HEADS, HEAD_W), F32)
    vs_all = jnp.zeros((DEPTH, dbsz * dseq * N_HEADS, HEAD_W), F32)

    w_in_b = w_in.astype(BF16)
    hp_l, hs_l = [], []
    for l in range(DEPTH):
        lam_init = 0.8 - 0.6 * math.exp(-0.3 * l)
        lam5 = jnp.concatenate([lambda_qk[l].astype(F32), jnp.full((1, HEAD_DIM), lam_init, F32)], axis=0)
        wp_b, wa_b, wb_b, wo_b = (w_pool[l].astype(BF16), w_a[l].astype(BF16),
                                  w_b[l].astype(BF16), w_o[l].astype(BF16))
        ps = pool_scale[l].reshape(1, D_POOL)
        subw = subln_w[l].reshape(1, HEAD_W)
        g = ln_g[l].reshape(1, D_MODEL)
        b = ln_b[l].reshape(1, D_MODEL)

        px, pg, q, kb, vb, ag, ga, gb, kp_all, vp_all = _in_projection(
            l, xp, w_in_b, cos_p, sin_p, kp_all, vp_all)
        r3 = lambda a: a.reshape(bsz, seq, a.shape[-1])
        o = _prompt_attention(r3(q), r3(kb), r3(vb), lam5, subw)
        px3 = r3(px)
        xp = _out_projection(px3, px3, r3(pg), o, r3(ag), r3(ga), r3(gb), r3(xp),
                             wp_b, ps, wa_b, wb_b, wo_b, g, b,
                             nb=1, tm=ROW_TILE, pos0=0,
                             first_tile_has_no_history=True).reshape(bsz * seq, D_MODEL)
        hp_l.append(px3[:, seq - POOL_HIST:, :])

        px, pg, q, kb, vb, ag, ga, gb, ks_all, vs_all = _in_projection(
            l, xs, w_in_b, cos_s, sin_s, ks_all, vs_all)
        s3 = lambda a: a.reshape(dbsz, dseq, a.shape[-1])
        o = _sample_attention(l, s3(q), ks_all, vs_all, cache_k, cache_v, lam5, subw)
        px3 = s3(px)
        xs = _out_projection(px3, state_rows[l], s3(pg), o, s3(ag), s3(ga), s3(gb), s3(xs),
                             wp_b, ps, wa_b, wb_b, wo_b, g, b,
                             nb=dbsz, tm=dseq, pos0=past,
                             first_tile_has_no_history=False).reshape(dbsz * dseq, D_MODEL)
        hs_l.append(jnp.concatenate([state_pool[l], px3], axis=1)[:, -POOL_HIST:, :])

    return (xp.reshape(bsz, seq, D_MODEL), xs.reshape(dbsz, dseq, D_MODEL),
            kp_all.reshape(DEPTH, bsz, seq, N_HEADS, HEAD_W),
            vp_all.reshape(DEPTH, bsz, seq, N_HEADS, HEAD_W),
            jnp.stack(hp_l),
            ks_all.reshape(DEPTH, dbsz, dseq, N_HEADS, HEAD_W),
            vs_all.reshape(DEPTH, dbsz, dseq, N_HEADS, HEAD_W),
            jnp.stack(hs_l))
```

```python
import functools
import math

import jax
import jax.numpy as jnp
from jax import lax
from jax.experimental import pallas as pl
from jax.experimental.pallas import tpu as pltpu

F32 = jnp.float32
BF16 = jnp.bfloat16

D_MODEL = 1024
DEPTH = 4
CHUNK = 64
CHUNK_SHIFT = 6
D_POOL = 512
POOL_WINDOWS = (2, 4, 8, 16)
POOL_GROUP = 128
POOL_HIST = 15
HIST_ROWS = 32
N_HEADS = 8
HEAD_DIM = 64
HEAD_W = 2 * HEAD_DIM
ROPE_HALF = HEAD_DIM // 2
ROPE_THETA = 10000.0
QK_SCALE = HEAD_DIM ** -0.5
Q_PRESCALE = QK_SCALE * math.log2(math.e)
LN_EPS = 1e-5
RMS_EPS = 1e-5
ALPHA = (2 * DEPTH) ** 0.25
COL_PX, COL_PG, COL_Q, COL_K, COL_V, COL_AG, COL_GA, COL_GB = 0, 512, 1024, 2048, 3072, 4096, 5120, 6144
D_IN = 7168

V7X_VMEM_BYTES = 64 * 1024 * 1024
LANES = 128

MASKED = -1e30
ROW_TILE = 512
ATTN_TILE = 512
ATTN_WIDE_KEYS = 1024
ATTN_HEADS_PER_STEP = 4
TEMP_VMEM_BYTES = 16 << 20


def _vmem_limit(pipelined_bytes, resident_bytes=0):
    return min(2 * pipelined_bytes + resident_bytes + TEMP_VMEM_BYTES, V7X_VMEM_BYTES - (4 << 20))


def _sigmoid(x):
    return 1.0 / (1.0 + jnp.exp(-x))


def _silu(x):
    return x * _sigmoid(x)


def _layer_norm_rows(z, g, b):
    mu = jnp.mean(z, axis=-1, keepdims=True)
    zc = z - mu
    var = jnp.mean(zc * zc, axis=-1, keepdims=True)
    return zc * lax.rsqrt(var + LN_EPS) * g + b


def _ln_kernel(x_ref, g_ref, b_ref, y_ref):
    y_ref[...] = _layer_norm_rows(x_ref[...], g_ref[...], b_ref[...])


def _entry_layer_norm(x2d, g, b):
    rows = x2d.shape[0]
    tm = min(ROW_TILE, rows)
    row_spec = pl.BlockSpec((tm, D_MODEL), lambda i: (i, 0))
    vec_spec = pl.BlockSpec((1, D_MODEL), lambda i: (0, 0))
    return pl.pallas_call(
        _ln_kernel,
        grid=(rows // tm,),
        in_specs=[row_spec, vec_spec, vec_spec],
        out_specs=row_spec,
        out_shape=jax.ShapeDtypeStruct((rows, D_MODEL), F32),
        compiler_params=pltpu.CompilerParams(
            dimension_semantics=("arbitrary",),
            vmem_limit_bytes=_vmem_limit(2 * tm * D_MODEL * 4)),
        name="entry_ln",
    )(x2d, g.reshape(1, D_MODEL), b.reshape(1, D_MODEL))


def _rope_table_kernel(cos_ref, sin_ref, *, pos0, period, tm):
    i = pl.program_id(0)
    lane = lax.broadcasted_iota(jnp.int32, (tm, LANES), 1)
    row = lax.broadcasted_iota(jnp.int32, (tm, LANES), 0) + i * tm
    freq = (lane & (ROPE_HALF - 1)).astype(F32)
    inv = jnp.exp(freq * (-math.log(ROPE_THETA) / ROPE_HALF))
    pos = (pos0 + (row & (period - 1))).astype(F32)
    ang = pos * inv
    sign = jnp.where((lane & ROPE_HALF) == 0, -1.0, 1.0)
    cos_ref[...] = jnp.cos(ang)
    sin_ref[...] = jnp.sin(ang) * sign


def _rope_tables(rows, pos0, period):
    tm = min(ROW_TILE, rows)
    spec = pl.BlockSpec((tm, LANES), lambda i: (i, 0))
    shape = jax.ShapeDtypeStruct((rows, LANES), F32)
    return pl.pallas_call(
        functools.partial(_rope_table_kernel, pos0=pos0, period=period, tm=tm),
        grid=(rows // tm,),
        out_specs=[spec, spec],
        out_shape=[shape, shape],
        compiler_params=pltpu.CompilerParams(dimension_semantics=("arbitrary",)),
        name="rope_tables",
    )()


def _inproj_kernel(layer_ref, x_ref, w_ref, cos_ref, sin_ref, k_all_in, v_all_in,
                   px_ref, pg_ref, q_ref, kb_ref, vb_ref, ag_ref, ga_ref, gb_ref, k_ref, v_ref,
                   xb_ref):
    del layer_ref, k_all_in, v_all_in
    tm = x_ref.shape[0]
    xb_ref[...] = x_ref[...].astype(BF16)

    def proj(col, width):
        return jnp.dot(xb_ref[...], w_ref[:, col:col + width], preferred_element_type=F32)

    def head_rows(h):
        return pl.ds(h, tm, stride=N_HEADS)

    px_ref[...] = proj(COL_PX, D_POOL)
    pg_ref[...] = proj(COL_PG, D_POOL).astype(BF16)

    cos = cos_ref[...]
    sin = sin_ref[...]
    lane = lax.broadcasted_iota(jnp.int32, (tm, LANES), 1)
    holds_x1 = (lane & ROPE_HALF) == 0

    def rope(y):
        partner = jnp.where(holds_x1,
                            pltpu.roll(y, LANES - ROPE_HALF, 1),
                            pltpu.roll(y, ROPE_HALF, 1))
        return y * cos + partner * sin

    q = proj(COL_Q, D_MODEL)
    for h in range(N_HEADS):
        sl = slice(h * HEAD_W, (h + 1) * HEAD_W)
        q_ref[:, sl] = (rope(q[:, sl]) * Q_PRESCALE).astype(BF16)
    k = proj(COL_K, D_MODEL)
    for h in range(N_HEADS):
        sl = slice(h * HEAD_W, (h + 1) * HEAD_W)
        kr = rope(k[:, sl])
        k_ref[0, head_rows(h), :] = kr
        kb_ref[:, sl] = kr.astype(BF16)
    v = proj(COL_V, D_MODEL)
    for h in range(N_HEADS):
        sl = slice(h * HEAD_W, (h + 1) * HEAD_W)
        v_ref[0, head_rows(h), :] = v[:, sl]
    vb_ref[...] = v.astype(BF16)
    ag_ref[...] = proj(COL_AG, D_MODEL).astype(BF16)
    ga_ref[...] = proj(COL_GA, D_MODEL).astype(BF16)
    gb_ref[...] = proj(COL_GB, D_MODEL).astype(BF16)


def _in_projection(layer, x2d, w_bf16, cos_t, sin_t, k_all, v_all):
    rows = x2d.shape[0]
    slab_shape = jax.ShapeDtypeStruct((DEPTH, rows * N_HEADS, HEAD_W), F32)
    tm = min(ROW_TILE, rows)
    table_blocks = cos_t.shape[0] // tm

    def rows_spec(width):
        return pl.BlockSpec((tm, width), lambda i, layer_ref: (i, 0))

    table_spec = pl.BlockSpec((tm, LANES), lambda i, layer_ref: (i % table_blocks, 0))
    w_spec = pl.BlockSpec((None, D_MODEL, D_IN), lambda i, layer_ref: (layer_ref[0], 0, 0),
                          pipeline_mode=pl.Buffered(1))
    kv_out_spec = pl.BlockSpec((1, tm * N_HEADS, HEAD_W), lambda i, layer_ref: (layer_ref[0], i, 0))
    any_spec = pl.BlockSpec(memory_space=pl.ANY)
    outs = [(D_POOL, F32), (D_POOL, BF16), (D_MODEL, BF16), (D_MODEL, BF16), (D_MODEL, BF16),
            (D_MODEL, BF16), (D_MODEL, BF16), (D_MODEL, BF16)]
    out_row_bytes = sum(wd * jnp.dtype(dt).itemsize for wd, dt in outs) + 2 * D_MODEL * 4
    block_bytes = tm * (D_MODEL * 4 + 2 * LANES * 4 + out_row_bytes)
    res = pl.pallas_call(
        _inproj_kernel,
        grid_spec=pltpu.PrefetchScalarGridSpec(
            num_scalar_prefetch=1,
            grid=(rows // tm,),
            in_specs=[rows_spec(D_MODEL), w_spec, table_spec, table_spec, any_spec, any_spec],
            out_specs=[rows_spec(wd) for wd, _ in outs] + [kv_out_spec, kv_out_spec],
            scratch_shapes=[pltpu.VMEM((tm, D_MODEL), BF16)]),
        out_shape=[jax.ShapeDtypeStruct((rows, wd), dt) for wd, dt in outs] + [slab_shape, slab_shape],
        input_output_aliases={5: 8, 6: 9},
        compiler_params=pltpu.CompilerParams(
            dimension_semantics=("arbitrary",),
            vmem_limit_bytes=_vmem_limit(block_bytes, D_MODEL * D_IN * 2 + tm * D_MODEL * 2)),
        name="in_projection",
    )(jnp.full((1,), layer, jnp.int32), x2d, w_bf16, cos_t, sin_t, k_all, v_all)
    return res


def _stack_maps(q, q2_ref):
    t = q.shape[0]
    lane = lax.broadcasted_iota(jnp.int32, q.shape, 1)
    zero = jnp.zeros_like(q)
    q2_ref[0:t, :] = jnp.where(lane < HEAD_DIM, q, zero)
    q2_ref[t:2 * t, :] = jnp.where(lane >= HEAD_DIM, q, zero)


def _scores(q2, k_blk):
    return lax.dot_general(q2, k_blk, (((1,), (1,)), ((), ())), preferred_element_type=F32)


def _lambda(lam_ref):
    lq = lam_ref[...]
    d1 = jnp.sum(lq[0:1, :] * lq[1:2, :], axis=1, keepdims=True)
    d2 = jnp.sum(lq[2:3, :] * lq[3:4, :], axis=1, keepdims=True)
    lam_init = lq[4:5, 0:1]
    return jnp.exp(d1) - jnp.exp(d2) + lam_init, lam_init


def _finish_head(acc, l, t, lam, lam_init, subw):
    o1 = acc[0:t, :] / l[0:t, :]
    o2 = acc[t:2 * t, :] / l[t:2 * t, :]
    o = o1 - lam * o2
    ms = jnp.mean(o * o, axis=1, keepdims=True)
    return o * lax.rsqrt(ms + RMS_EPS) * subw * (1.0 - lam_init)


def _prompt_attn_kernel(lam_ref, subw_ref, q_ref, k_ref, v_ref, o_ref,
                        q2_ref, vx_ref, m_ref, acc_ref, *, t, heads):
    qi = pl.program_id(2)
    hf = t // 2

    @pl.when(qi == 0)
    def _():
        for g in range(heads):
            vx_ref[g, :, 0:HEAD_W] = v_ref[0, :, g * HEAD_W:(g + 1) * HEAD_W]
            vx_ref[g, :, HEAD_W:2 * HEAD_W] = jnp.ones((vx_ref.shape[1], HEAD_W), BF16)

    for g in range(heads):
        lanes = slice(g * HEAD_W, (g + 1) * HEAD_W)
        _stack_maps(q_ref[0, 0:hf, lanes], q2_ref.at[g, 0:t])
        _stack_maps(q_ref[0, hf:t, lanes], q2_ref.at[g, t:2 * t])
    m_ref[...] = jnp.full(m_ref.shape, MASKED, F32)
    acc_ref[...] = jnp.zeros(acc_ref.shape, F32)

    half_row_chunk = (lax.broadcasted_iota(jnp.int32, (t, LANES), 0) & (hf - 1)) >> CHUNK_SHIFT
    lane_chunk = lax.broadcasted_iota(jnp.int32, (t, LANES), 1) >> CHUNK_SHIFT
    chunks_per_slab = LANES // CHUNK
    own_hidden = [lane_chunk + cb * chunks_per_slab > half_row_chunk for cb in range(hf // LANES)]

    def head_step(g, off, width, row0=0, nrows=2 * t, masked_from=None):
        lanes = slice(g * HEAD_W, (g + 1) * HEAD_W)
        rows = slice(row0, row0 + nrows)
        s = _scores(q2_ref[g, rows], k_ref[0, pl.ds(off, width), lanes])
        slabs = [s[:, cb * LANES:(cb + 1) * LANES] for cb in range(width // LANES)]
        if masked_from is not None:
            slabs = slabs[:masked_from] + [
                jnp.where(own_hidden[cb - masked_from], MASKED, slabs[cb])
                for cb in range(masked_from, len(slabs))]
        top = slabs[0]
        for sb in slabs[1:]:
            top = jnp.maximum(top, sb)
        m_prev = m_ref[g, rows]
        m_new = jnp.maximum(m_prev, jnp.max(top, axis=1, keepdims=True))
        alpha = jnp.exp2(m_prev - m_new)
        p = jnp.concatenate([jnp.exp2(sb - m_new) for sb in slabs], axis=1).astype(BF16)
        m_ref[g, rows] = m_new
        pv = jnp.dot(p, vx_ref[g, pl.ds(off, width), :], preferred_element_type=F32)
        for part in range(2):
            cols = slice(part * HEAD_W, (part + 1) * HEAD_W)
            acc_ref[g, rows, cols] = alpha * acc_ref[g, rows, cols] + pv[:, cols]

    def block_step(off, width):
        for g in range(heads):
            head_step(g, off, width)

    ww = ATTN_WIDE_KEYS
    assert ww in (t, 2 * t)
    n_wide = lax.shift_right_logical(qi * t, ww.bit_length() - 1)

    def wide_pair(f, carry):
        block_step(pl.multiple_of(f * 2 * ww, 2 * ww), ww)
        block_step(pl.multiple_of(f * 2 * ww + ww, ww), ww)
        return carry

    lax.fori_loop(0, lax.shift_right_logical(n_wide, 1), wide_pair, 0)

    @pl.when((n_wide & 1) == 1)
    def _():
        block_step(pl.multiple_of((n_wide - 1) * ww, ww), ww)

    if t < ww:
        @pl.when((qi & 1) == 1)
        def _():
            block_step(pl.multiple_of((qi - 1) * t, t), t)

    own = pl.multiple_of(qi * t, t)
    slabs_per_half = hf // LANES
    for g in range(heads):
        head_step(g, own, hf, row0=0, nrows=t, masked_from=0)
        head_step(g, own, t, row0=t, nrows=t, masked_from=slabs_per_half)

    lam, lam_init = _lambda(lam_ref)
    for g in range(heads):
        for half in range(2):
            rows = slice(half * t, (half + 1) * t)
            o_ref[0, half * hf:(half + 1) * hf, g * HEAD_W:(g + 1) * HEAD_W] = _finish_head(
                acc_ref[g, rows, 0:HEAD_W], acc_ref[g, rows, HEAD_W:2 * HEAD_W],
                hf, lam, lam_init, subw_ref[...]).astype(BF16)


def _prompt_attention(q, kb, vb, lam5, subw):
    bsz, seq, width = q.shape
    t = ATTN_TILE
    heads = ATTN_HEADS_PER_STEP
    q_spec = pl.BlockSpec((1, t, heads * HEAD_W), lambda b, h, i: (b, i, h))
    kv_spec = pl.BlockSpec((1, seq, heads * HEAD_W), lambda b, h, i: (b, 0, h))
    block_bytes = heads * (2 * t * HEAD_W * 2 + 2 * seq * HEAD_W * 2)
    scratch_bytes = heads * (2 * t * HEAD_W * 2 + seq * 2 * HEAD_W * 2 + 2 * t * (LANES + 2 * HEAD_W) * 4)
    return pl.pallas_call(
        functools.partial(_prompt_attn_kernel, t=t, heads=heads),
        grid=(bsz, N_HEADS // heads, seq // t),
        in_specs=[pl.BlockSpec((5, HEAD_DIM), lambda b, h, i: (0, 0)),
                  pl.BlockSpec((1, HEAD_W), lambda b, h, i: (0, 0)),
                  q_spec, kv_spec, kv_spec],
        out_specs=q_spec,
        out_shape=jax.ShapeDtypeStruct((bsz, seq, width), BF16),
        scratch_shapes=[pltpu.VMEM((heads, 2 * t, HEAD_W), BF16),
                        pltpu.VMEM((heads, seq, 2 * HEAD_W), BF16),
                        pltpu.VMEM((heads, 2 * t, LANES), F32),
                        pltpu.VMEM((heads, 2 * t, 2 * HEAD_W), F32)],
        compiler_params=pltpu.CompilerParams(
            dimension_semantics=("arbitrary", "arbitrary", "arbitrary"),
            vmem_limit_bytes=_vmem_limit(block_bytes, scratch_bytes)),
        name="prompt_attention",
    )(lam5, subw, q, kb, vb)


def _sample_attn_kernel(layer_ref, lam_ref, subw_ref, q_ref, kn_ref, vn_ref, kc_ref, vc_ref, o_ref,
                        q2_ref, *, t):
    del layer_ref
    rows_per_head = 2 * t
    rows = N_HEADS * rows_per_head
    for h in range(N_HEADS):
        _stack_maps(q_ref[0, :, h * HEAD_W:(h + 1) * HEAD_W],
                    q2_ref.at[h * rows_per_head:(h + 1) * rows_per_head])
    q2 = q2_ref[...]
    s_c = _scores(q2, kc_ref[0, 0].astype(BF16))
    s_n = _scores(q2, kn_ref[0].astype(BF16))

    r = lax.broadcasted_iota(jnp.int32, (rows, LANES), 0)
    c = lax.broadcasted_iota(jnp.int32, (rows, LANES), 1)
    head_shift = rows_per_head.bit_length() - 1
    other_head = (c & (N_HEADS - 1)) != (r >> head_shift)

    def slabs_of(s):
        return [jnp.where(other_head, MASKED, s[:, cb * LANES:(cb + 1) * LANES])
                for cb in range(s.shape[1] // LANES)]

    slabs_c = slabs_of(s_c)
    slabs_n = slabs_of(s_n)
    top = slabs_n[0]
    for sb in slabs_n[1:] + slabs_c:
        top = jnp.maximum(top, sb)
    m = jnp.max(top, axis=1, keepdims=True)
    ps_c = [jnp.exp2(sb - m) for sb in slabs_c]
    ps_n = [jnp.exp2(sb - m) for sb in slabs_n]
    psum = ps_n[0]
    for pc in ps_n[1:] + ps_c:
        psum = psum + pc
    l_all = jnp.sum(psum, axis=1, keepdims=True)
    acc_all = (jnp.dot(jnp.concatenate(ps_c, axis=1).astype(BF16), vc_ref[0, 0].astype(BF16),
                       preferred_element_type=F32)
               + jnp.dot(jnp.concatenate(ps_n, axis=1).astype(BF16), vn_ref[0].astype(BF16),
                         preferred_element_type=F32))

    lam, lam_init = _lambda(lam_ref)
    subw = subw_ref[...]
    for h in range(N_HEADS):
        hr = slice(h * rows_per_head, (h + 1) * rows_per_head)
        o_ref[0, :, h * HEAD_W:(h + 1) * HEAD_W] = _finish_head(
            acc_all[hr, :], l_all[hr, :], t, lam, lam_init, subw).astype(BF16)


def _sample_attention(layer, q, k_new, v_new, cache_k, cache_v, lam5, subw):
    bsz, t, width = q.shape
    past = cache_k.shape[2]
    q_spec = pl.BlockSpec((1, t, width), lambda b, layer_ref: (b, 0, 0))
    new_spec = pl.BlockSpec((1, t * N_HEADS, HEAD_W), lambda b, layer_ref: (layer_ref[0], b, 0))
    cache_spec = pl.BlockSpec((1, 1, past * N_HEADS, HEAD_W),
                              lambda b, layer_ref: (layer_ref[0], b, 0, 0))
    cache_k2 = cache_k.reshape(DEPTH, bsz, past * N_HEADS, HEAD_W)
    cache_v2 = cache_v.reshape(DEPTH, bsz, past * N_HEADS, HEAD_W)
    rows = 2 * t * N_HEADS
    block_bytes = 2 * t * width * 2 + 2 * t * width * 4 + 2 * past * width * 4
    temp_bytes = rows * past * N_HEADS * (4 + 2) + 2 * past * width * 2
    return pl.pallas_call(
        functools.partial(_sample_attn_kernel, t=t),
        grid_spec=pltpu.PrefetchScalarGridSpec(
            num_scalar_prefetch=1,
            grid=(bsz,),
            in_specs=[pl.BlockSpec((5, HEAD_DIM), lambda b, layer_ref: (0, 0)),
                      pl.BlockSpec((1, HEAD_W), lambda b, layer_ref: (0, 0)),
                      q_spec, new_spec, new_spec, cache_spec, cache_spec],
            out_specs=q_spec,
            scratch_shapes=[pltpu.VMEM((rows, HEAD_W), BF16)]),
        out_shape=jax.ShapeDtypeStruct((bsz, t, width), BF16),
        compiler_params=pltpu.CompilerParams(
            dimension_semantics=("arbitrary",),
            vmem_limit_bytes=_vmem_limit(block_bytes, temp_bytes)),
        name="sample_attention",
    )(jnp.full((1,), layer, jnp.int32), lam5, subw, q, k_new, v_new, cache_k2, cache_v2)


def _out_kernel(layer_ref, px_ref, hist_ref, pg_ref, o_ref, ag_ref, ga_ref, gb_ref, x_ref,
                wp_ref, ps_ref, wa_ref, wb_ref, wo_ref, g_ref, b_ref,
                y_ref, ext_ref, lvl_ref, *, nb, tm, pos0, first_tile_has_no_history):
    del layer_ref
    i = pl.program_id(1)
    rows = nb * tm
    hist = hist_ref[...]
    if first_tile_has_no_history:
        hist = jnp.where(i == 0, 0.0, hist)
    ext_ref[:, 0:HIST_ROWS, :] = hist
    ext_ref[:, HIST_ROWS:HIST_ROWS + tm, :] = px_ref[...]

    lo = HIST_ROWS // 2
    n = HIST_ROWS - lo + tm
    max_shift = max(POOL_WINDOWS) // 2
    lvl_ref[:, :, lo - max_shift:lo, :] = jnp.zeros((2, nb, max_shift, POOL_GROUP), F32)

    pos = pos0 + i * tm + lax.broadcasted_iota(jnp.int32, (nb, tm, POOL_GROUP), 1)
    ya_parts = []
    for g, w in enumerate(POOL_WINDOWS):
        sl = slice(g * POOL_GROUP, (g + 1) * POOL_GROUP)
        u = ext_ref[:, HIST_ROWS:HIST_ROWS + tm, sl]
        levels = w.bit_length() - 1
        for k in range(levels):
            sh = 1 << k
            if k == 0:
                win = ext_ref[:, lo:lo + n, sl] + ext_ref[:, lo - sh:lo - sh + n, sl]
            else:
                src = lvl_ref.at[(k - 1) % 2]
                win = src[:, lo:lo + n, :] + src[:, lo - sh:lo - sh + n, :]
            if k < levels - 1:
                lvl_ref[k % 2, :, lo:lo + n, :] = win
        s = win[:, HIST_ROWS - lo:, :]
        cnt = jnp.minimum(pos + 1, w).astype(F32)
        pooled = (s / cnt - u).reshape(rows, POOL_GROUP)
        ya_parts.append(jnp.dot(pooled.astype(BF16), wp_ref[g], preferred_element_type=F32))
    ya = (jnp.concatenate(ya_parts, axis=1) * ps_ref[...]).astype(BF16)
    ya = ya * _silu(pg_ref[...].reshape(rows, D_POOL))
    yb = o_ref[...].reshape(rows, D_MODEL) * _silu(ag_ref[...].reshape(rows, D_MODEL))

    ma = jnp.dot(ya, wa_ref[...], preferred_element_type=F32)
    mb = jnp.dot(yb, wb_ref[...], preferred_element_type=F32)
    merged = (_sigmoid(ga_ref[...].reshape(rows, D_MODEL)) * ma.astype(BF16)
              + _sigmoid(gb_ref[...].reshape(rows, D_MODEL)) * mb.astype(BF16))
    out = jnp.dot(merged, wo_ref[...], preferred_element_type=F32)
    z = ALPHA * x_ref[...].reshape(rows, D_MODEL) + out
    y_ref[...] = _layer_norm_rows(z, g_ref[...], b_ref[...]).reshape(nb, tm, D_MODEL)


def _out_projection(layer, px, hist, pg, o, ag, ga, gb, x, wp, ps, wa, wb, wo, g, b, *,
                    nb, tm, pos0, first_tile_has_no_history):
    bsz, seq, _ = px.shape
    hist_blocks_per_tile = tm // HIST_ROWS

    def act_spec(width):
        return pl.BlockSpec((nb, tm, width), lambda bi, i, layer_ref: (bi, i, 0))

    def whole(shape):
        return pl.BlockSpec(shape, lambda bi, i, layer_ref: (0,) * len(shape),
                            pipeline_mode=pl.Buffered(1))

    def of_layer(shape):
        return pl.BlockSpec((None,) + shape, lambda bi, i, layer_ref: (layer_ref[0],) + (0,) * len(shape),
                            pipeline_mode=pl.Buffered(1))

    hist_spec = pl.BlockSpec(
        (nb, HIST_ROWS, D_POOL),
        lambda bi, i, layer_ref: (bi, jnp.maximum(i * hist_blocks_per_tile - 1, 0), 0))
    rows = nb * tm
    block_bytes = rows * (D_POOL * 4 + D_POOL * 2 + 4 * D_MODEL * 2 + 2 * D_MODEL * 4)
    weight_bytes = (D_POOL + 2 * D_MODEL) * D_MODEL * 2 + rows * D_POOL * 4
    return pl.pallas_call(
        functools.partial(_out_kernel, nb=nb, tm=tm, pos0=pos0,
                          first_tile_has_no_history=first_tile_has_no_history),
        grid_spec=pltpu.PrefetchScalarGridSpec(
            num_scalar_prefetch=1,
            grid=(bsz // nb, seq // tm),
            in_specs=[act_spec(D_POOL), hist_spec, act_spec(D_POOL), act_spec(D_MODEL),
                      act_spec(D_MODEL), act_spec(D_MODEL), act_spec(D_MODEL), act_spec(D_MODEL),
                      of_layer((len(POOL_WINDOWS), POOL_GROUP, POOL_GROUP)), whole((1, D_POOL)),
                      of_layer((D_POOL, D_MODEL)), of_layer((D_MODEL, D_MODEL)),
                      of_layer((D_MODEL, D_MODEL)), whole((1, D_MODEL)), whole((1, D_MODEL))],
            out_specs=act_spec(D_MODEL),
            scratch_shapes=[pltpu.VMEM((nb, HIST_ROWS + tm, D_POOL), F32),
                            pltpu.VMEM((2, nb, HIST_ROWS + tm, POOL_GROUP), F32)]),
        out_shape=jax.ShapeDtypeStruct((bsz, seq, D_MODEL), F32),
        compiler_params=pltpu.CompilerParams(
            dimension_semantics=("arbitrary", "arbitrary"),
            vmem_limit_bytes=_vmem_limit(block_bytes, weight_bytes)),
        name="out_projection",
    )(jnp.full((1,), layer, jnp.int32), px, hist, pg, o, ag, ga, gb, x, wp, ps, wa, wb, wo, g, b)


def kernel(x_prompt, x_sample, cache_k, cache_v, state_pool, ln_in_g, ln_in_b, w_in, w_pool,
           pool_scale, lambda_qk, subln_w, w_a, w_b, w_o, ln_g, ln_b):
    bsz, seq, _ = x_prompt.shape
    dbsz, dseq, _ = x_sample.shape
    past = cache_k.shape[2]

    cos_p, sin_p = _rope_tables(seq, 0, seq)
    cos_s, sin_s = _rope_tables(dbsz * dseq, past, dseq)

    xp = _entry_layer_norm(x_prompt.reshape(bsz * seq, D_MODEL), ln_in_g, ln_in_b)
    xs = _entry_layer_norm(x_sample.reshape(dbsz * dseq, D_MODEL), ln_in_g, ln_in_b)

    state_rows = jnp.pad(state_pool, ((0, 0), (0, 0), (HIST_ROWS - POOL_HIST, 0), (0, 0)))

    kp_all = jnp.zeros((DEPTH, bsz * seq * N_HEADS, HEAD_W), F32)
    vp_all = jnp.zeros((DEPTH, bsz * seq * N_HEADS, HEAD_W), F32)
    ks_all = jnp.zeros((DEPTH, dbsz * dseq * N_HEADS, HEAD_W), F32)
    vs_all = jnp.zeros((DEPTH, dbsz * dseq * N_HEADS, HEAD_W), F32)

    w_in_b = w_in.astype(BF16)
    wp_b, wa_b, wb_b, wo_b = (w_pool.astype(BF16), w_a.astype(BF16), w_b.astype(BF16),
                              w_o.astype(BF16))
    hp_l, hs_l = [], []
    for l in range(DEPTH):
        lam_init = 0.8 - 0.6 * math.exp(-0.3 * l)
        lam5 = jnp.concatenate([lambda_qk[l].astype(F32), jnp.full((1, HEAD_DIM), lam_init, F32)], axis=0)
        ps = pool_scale[l].reshape(1, D_POOL)
        subw = subln_w[l].reshape(1, HEAD_W)
        g = ln_g[l].reshape(1, D_MODEL)
        b = ln_b[l].reshape(1, D_MODEL)

        px, pg, q, kb, vb, ag, ga, gb, kp_all, vp_all = _in_projection(
            l, xp, w_in_b, cos_p, sin_p, kp_all, vp_all)
        r3 = lambda a: a.reshape(bsz, seq, a.shape[-1])
        o = _prompt_attention(r3(q), r3(kb), r3(vb), lam5, subw)
        px3 = r3(px)
        xp = _out_projection(l, px3, px3, r3(pg), o, r3(ag), r3(ga), r3(gb), r3(xp),
                             wp_b, ps, wa_b, wb_b, wo_b, g, b,
                             nb=1, tm=ROW_TILE, pos0=0,
                             first_tile_has_no_history=True).reshape(bsz * seq, D_MODEL)
        hp_l.append(px3[:, seq - POOL_HIST:, :])

        px, pg, q, kb, vb, ag, ga, gb, ks_all, vs_all = _in_projection(
            l, xs, w_in_b, cos_s, sin_s, ks_all, vs_all)
        s3 = lambda a: a.reshape(dbsz, dseq, a.shape[-1])
        o = _sample_attention(l, s3(q), ks_all, vs_all, cache_k, cache_v, lam5, subw)
        px3 = s3(px)
        xs = _out_projection(l, px3, state_rows[l], s3(pg), o, s3(ag), s3(ga), s3(gb), s3(xs),
                             wp_b, ps, wa_b, wb_b, wo_b, g, b,
                             nb=dbsz, tm=dseq, pos0=past,
                             first_tile_has_no_history=False).reshape(dbsz * dseq, D_MODEL)
        hs_l.append(jnp.concatenate([state_pool[l], px3], axis=1)[:, -POOL_HIST:, :])

    return (xp.reshape(bsz, seq, D_MODEL), xs.reshape(dbsz, dseq, D_MODEL),
            kp_all.reshape(DEPTH, bsz, seq, N_HEADS, HEAD_W),
            vp_all.reshape(DEPTH, bsz, seq, N_HEADS, HEAD_W),
            jnp.stack(hp_l),
            ks_all.reshape(DEPTH, dbsz, dseq, N_HEADS, HEAD_W),
            vs_all.reshape(DEPTH, dbsz, dseq, N_HEADS, HEAD_W),
            jnp.stack(hs_l))
```
